```python
import math
import jax, jax.numpy as jnp
from jax import lax
import numpy as np


D_MODEL = 2048
BATCH = 4
SEQ = 8192
DEPTH = 1

Q_BLOCK = 128
HEAD_DIM = 128
SB_HEADS = 8
SB_WIDTH = SB_HEADS * HEAD_DIM
NSA_HEADS = 8
NSA_KV_GROUPS = 2
NSA_HPG = NSA_HEADS // NSA_KV_GROUPS
NSA_WIDTH = NSA_HEADS * HEAD_DIM
NSA_KV_WIDTH = NSA_KV_GROUPS * HEAD_DIM
CMP_LEN = 32
CMP_STRIDE = 16
CMP_HIDDEN = 256
SLC_BLOCK = 64
SLC_TOPN = 16
WINDOW = 512
REL_BUCKETS = 32
REL_MAX_DIST = 128
N_GROUPS = 8
EXPERTS_PER_GROUP = 8
N_EXPERTS = N_GROUPS * EXPERTS_PER_GROUP
TOPK_IN_GROUP = 2
EXPERT_HIDDEN = 1408
MOE_BLOCK = 128
RMS_EPS = 1e-6
FORCED_SCORE = 1e4
IN_SIZES = (SB_WIDTH, SB_WIDTH, SB_WIDTH, NSA_WIDTH,
            NSA_KV_WIDTH, NSA_KV_WIDTH, NSA_KV_WIDTH, NSA_KV_WIDTH, NSA_KV_WIDTH, NSA_KV_WIDTH,
            3 * NSA_HEADS, D_MODEL, D_MODEL)
IN_COLS = 3 * SB_WIDTH + NSA_WIDTH + 6 * NSA_KV_WIDTH + 3 * NSA_HEADS + 2 * D_MODEL

kernel_name = 'hybrid_stickbreak_nsa_hmoe'


def rmsnorm(x, g):
    xf = x.astype(jnp.float32)
    y = xf * lax.rsqrt(jnp.mean(xf * xf, axis=-1, keepdims=True) + RMS_EPS)
    return (y * g.astype(jnp.float32)).astype(x.dtype)


def modulate(xn, shift, scale):
    return xn * (1.0 + scale[:, None, :]) + shift[:, None, :]


def t5_bucket(dist):
    n = jnp.maximum(dist, 0)
    max_exact = REL_BUCKETS // 2
    nf = jnp.maximum(n, 1).astype(jnp.float32)
    large = max_exact + (jnp.log(nf / max_exact) / math.log(REL_MAX_DIST / max_exact)
                         * (REL_BUCKETS - max_exact)).astype(jnp.int32)
    large = jnp.minimum(large, REL_BUCKETS - 1)
    return jnp.where(n < max_exact, n, large)


def masked_softmax(s, mask):
    s = jnp.where(mask, s.astype(jnp.float32), -jnp.inf)
    m = jnp.max(s, axis=-1, keepdims=True)
    m = jnp.where(jnp.isfinite(m), m, 0.0)
    e = jnp.where(mask, jnp.exp(s - m), 0.0)
    return e / jnp.maximum(jnp.sum(e, axis=-1, keepdims=True), 1e-30)


def stick_breaking_attention(q, k, v):
    B, H, S, dh = q.shape
    scale = dh ** -0.5
    spos = jnp.arange(S)

    def block(i):
        t0 = i * Q_BLOCK
        qb = lax.dynamic_slice_in_dim(q, t0, Q_BLOCK, axis=2)
        z = jnp.einsum('bhqd,bhkd->bhqk', qb, k).astype(jnp.float32) * scale
        tpos = t0 + jnp.arange(Q_BLOCK)
        causal = spos[None, :] < tpos[:, None]
        log_1m = jnp.where(causal, jax.nn.log_sigmoid(-z), 0.0)
        after = lax.cumsum(log_1m, axis=3, reverse=True) - log_1m
        a = jnp.where(causal, jnp.exp(jax.nn.log_sigmoid(z) + after), 0.0)
        return jnp.einsum('bhqk,bhkd->bhqd', a.astype(v.dtype), v)

    o = lax.map(block, jnp.arange(S // Q_BLOCK))
    return o.transpose(1, 0, 3, 2, 4).reshape(B, S, H * dh)


def compress_blocks(kv, pe, w1, w2):
    B, G, S, dh = kv.shape
    chunks = kv.reshape(B, G, S // CMP_STRIDE, CMP_STRIDE, dh)
    blocks = jnp.concatenate([chunks[:, :, :-1], chunks[:, :, 1:]], axis=3) + pe
    flat = blocks.reshape(B, G, S // CMP_STRIDE - 1, CMP_LEN * dh)
    return jax.nn.gelu(flat @ w1) @ w2


def nsa_attention(q, k_cmp, v_cmp, k_slc, v_slc, k_swa, v_swa, gates, rel_bias):
    B, G, HPG, S, dh = q.shape
    scale = dh ** -0.5
    n_cmp = k_cmp.shape[2]
    n_slc = S // SLC_BLOCK
    n_sel = min(SLC_TOPN, n_slc)
    tbl = rel_bias.astype(jnp.float32).reshape(REL_BUCKETS, G, HPG)
    tbl_t = tbl.transpose(1, 2, 0)
    cmp_start = jnp.arange(n_cmp) * CMP_STRIDE
    cmp_end = cmp_start + CMP_LEN - 1
    slc_start = jnp.arange(n_slc) * SLC_BLOCK
    lo = jnp.maximum(cmp_start[:, None], slc_start[None, :])
    hi = jnp.minimum(cmp_start[:, None] + CMP_LEN, slc_start[None, :] + SLC_BLOCK)
    overlap = jnp.clip(hi - lo, 0).astype(jnp.float32) / CMP_LEN
    ks_blk = k_slc.reshape(B, G, n_slc, SLC_BLOCK, dh)
    vs_blk = v_slc.reshape(B, G, n_slc, SLC_BLOCK, dh)
    pad = ((0, 0), (0, 0), (WINDOW, 0), (0, 0))
    kw_pad = jnp.pad(k_swa, pad)
    vw_pad = jnp.pad(v_swa, pad)
    b_ix = jnp.arange(B)[:, None, None, None]
    g_ix = jnp.arange(G)[None, :, None, None]
    g6 = jnp.arange(G).reshape(1, G, 1, 1, 1, 1)
    h6 = jnp.arange(HPG).reshape(1, 1, HPG, 1, 1, 1)
    slc_offs = jnp.arange(SLC_BLOCK)
    blk_ids = jnp.arange(n_slc)

    def block(i):
        t0 = i * Q_BLOCK
        tpos = t0 + jnp.arange(Q_BLOCK)
        qb = lax.dynamic_slice_in_dim(q, t0, Q_BLOCK, axis=3)
        dist_c = tpos[:, None] - cmp_end[None, :]
        bias_c = tbl[t5_bucket(dist_c)].transpose(2, 3, 0, 1)
        s_c = jnp.einsum('bghqd,bgnd->bghqn', qb, k_cmp).astype(jnp.float32) * scale + bias_c
        p_c = masked_softmax(s_c, dist_c >= 0)
        o_c = jnp.einsum('bghqn,bgnd->bghqd', p_c.astype(v_cmp.dtype), v_cmp)
        imp = jnp.einsum('bghqn,nj->bgqj', p_c, overlap)
        cur = tpos // SLC_BLOCK
        forced = ((blk_ids[None, :] == 0) | (blk_ids[None, :] == cur[:, None])
                  | (blk_ids[None, :] == cur[:, None] - 1))
        valid_b = slc_start[None, :] <= tpos[:, None]
        score = jnp.where(forced, FORCED_SCORE, jnp.where(valid_b, imp, -jnp.inf))
        _, idx = lax.top_k(score, n_sel)
        k_sel = ks_blk[b_ix, g_ix, idx]
        v_sel = vs_blk[b_ix, g_ix, idx]
        pos = idx[..., None] * SLC_BLOCK + slc_offs
        dist_s = tpos[:, None, None] - pos
        bias_s = tbl_t[g6, h6, t5_bucket(dist_s)[:, :, None]]
        s_s = jnp.einsum('bghqd,bgqkld->bghqkl', qb, k_sel).astype(jnp.float32) * scale + bias_s
        n_tok = n_sel * SLC_BLOCK
        p_s = masked_softmax(s_s.reshape(B, G, HPG, Q_BLOCK, n_tok),
                             (dist_s >= 0)[:, :, None].reshape(B, G, 1, Q_BLOCK, n_tok))
        o_s = jnp.einsum('bghqn,bgqnd->bghqd', p_s.astype(v_slc.dtype),
                         v_sel.reshape(B, G, Q_BLOCK, n_tok, dh))
        kw = lax.dynamic_slice_in_dim(kw_pad, t0, WINDOW + Q_BLOCK, axis=2)
        vw = lax.dynamic_slice_in_dim(vw_pad, t0, WINDOW + Q_BLOCK, axis=2)
        wpos = t0 - WINDOW + jnp.arange(WINDOW + Q_BLOCK)
        dist_w = tpos[:, None] - wpos[None, :]
        valid_w = (dist_w >= 0) & (dist_w < WINDOW) & (wpos[None, :] >= 0)
        bias_w = tbl[t5_bucket(dist_w)].transpose(2, 3, 0, 1)
        s_w = jnp.einsum('bghqd,bgkd->bghqk', qb, kw).astype(jnp.float32) * scale + bias_w
        p_w = masked_softmax(s_w, valid_w)
        o_w = jnp.einsum('bghqk,bgkd->bghqd', p_w.astype(v_swa.dtype), vw)
        g = lax.dynamic_slice_in_dim(gates, t0, Q_BLOCK, axis=3)
        return g[..., 0:1] * o_c + g[..., 1:2] * o_s + g[..., 2:3] * o_w

    o = lax.map(block, jnp.arange(S // Q_BLOCK))
    return o.transpose(1, 0, 4, 2, 3, 5).reshape(B, S, G * HPG * dh)


def hierarchical_moe(h, w_grp, b_grp, w_exp, b_exp, w1, w3, w2):
    B, S, D = h.shape
    T = B * S
    xt = h.reshape(T, D)
    lg = (xt @ w_grp).astype(jnp.float32) + b_grp.astype(jnp.float32)
    pg = jax.nn.softmax(lg, axis=-1)
    pg_sel, g_sel = lax.top_k(pg, 1)
    le = ((xt @ w_exp).astype(jnp.float32) + b_exp.astype(jnp.float32)).reshape(T, N_GROUPS, EXPERTS_PER_GROUP)
    le = jnp.take_along_axis(le, g_sel[:, :, None], axis=1)[:, 0]
    pe = jax.nn.softmax(le, axis=-1)
    w_top, i_top = lax.top_k(pe, TOPK_IN_GROUP)
    w_top = w_top / jnp.sum(w_top, axis=-1, keepdims=True) * pg_sel
    e_id = g_sel * EXPERTS_PER_GROUP + i_top
    N = T * TOPK_IN_GROUP
    e_flat = e_id.reshape(N)
    tok = jnp.repeat(jnp.arange(T, dtype=jnp.int32), TOPK_IN_GROUP)
    w_flat = w_top.reshape(N)
    order = jnp.argsort(e_flat)
    e_sorted = e_flat[order]
    counts = jnp.bincount(e_flat, length=N_EXPERTS)
    start = jnp.cumsum(counts) - counts
    padded = (counts + MOE_BLOCK - 1) // MOE_BLOCK * MOE_BLOCK
    pend = jnp.cumsum(padded)
    pstart = pend - padded
    dest = pstart[e_sorted] + (jnp.arange(N) - start[e_sorted])
    n_rows = (N + N_EXPERTS * (MOE_BLOCK - 1) + MOE_BLOCK - 1) // MOE_BLOCK * MOE_BLOCK
    n_blk = n_rows // MOE_BLOCK
    row_tok = jnp.full((n_rows,), T, jnp.int32).at[dest].set(tok[order])
    row_w = jnp.zeros((n_rows,), jnp.float32).at[dest].set(w_flat[order])
    blk_e = jnp.minimum(jnp.searchsorted(pend, jnp.arange(n_blk) * MOE_BLOCK, side='right'), N_EXPERTS - 1)
    x_pad = jnp.concatenate([xt, jnp.zeros((1, D), xt.dtype)], axis=0)

    def run(args):
        toks, wts, e = args
        xs = x_pad[toks]
        hid = jax.nn.silu(xs @ w1[e]) * (xs @ w3[e])
        return (hid @ w2[e]) * wts[:, None].astype(xs.dtype)

    y = lax.map(run, (row_tok.reshape(n_blk, MOE_BLOCK), row_w.reshape(n_blk, MOE_BLOCK), blk_e))
    out = jnp.zeros((T + 1, D), h.dtype).at[row_tok].add(y.reshape(n_rows, D).astype(h.dtype))[:T]
    return out.reshape(B, S, D)


def setup_inputs(seed: int = 0) -> dict:
    key = jax.random.key(seed)
    ks = jax.random.split(key, 26)
    f32 = jnp.float32
    L = DEPTH
    D = D_MODEL

    def nrm(k, shape, scale):
        return jax.random.normal(k, shape, f32) * scale

    flat_cmp = CMP_LEN * HEAD_DIM
    return {
        'x': nrm(ks[0], (BATCH, SEQ, D), 1.0),
        'c': nrm(ks[1], (BATCH, D), 1.0),
        'ada_w': nrm(ks[2], (L, D, 6 * D), 0.5 * D ** -0.5),
        'ada_b': nrm(ks[3], (L, 6 * D), 0.02),
        'norm1_g': 1.0 + nrm(ks[4], (L, D), 0.02),
        'norm2_g': 1.0 + nrm(ks[5], (L, D), 0.02),
        'normf_g': 1.0 + nrm(ks[6], (D,), 0.02),
        'w_in': nrm(ks[7], (L, D, IN_COLS), D ** -0.5),
        'rel_bias': nrm(ks[8], (REL_BUCKETS, NSA_HEADS), 0.5),
        'cmp_pe_k': nrm(ks[9], (L, CMP_LEN, HEAD_DIM), 0.1),
        'cmp_w1_k': nrm(ks[10], (L, flat_cmp, CMP_HIDDEN), flat_cmp ** -0.5),
        'cmp_w2_k': nrm(ks[11], (L, CMP_HIDDEN, HEAD_DIM), CMP_HIDDEN ** -0.5),
        'cmp_pe_v': nrm(ks[12], (L, CMP_LEN, HEAD_DIM), 0.1),
        'cmp_w1_v': nrm(ks[13], (L, flat_cmp, CMP_HIDDEN), flat_cmp ** -0.5),
        'cmp_w2_v': nrm(ks[14], (L, CMP_HIDDEN, HEAD_DIM), CMP_HIDDEN ** -0.5),
        'w_branch_a': nrm(ks[15], (L, SB_WIDTH, D), SB_WIDTH ** -0.5),
        'w_branch_b': nrm(ks[16], (L, NSA_WIDTH, D), NSA_WIDTH ** -0.5),
        'w_out': nrm(ks[17], (L, D, D), D ** -0.5),
        'router_w_grp': nrm(ks[18], (L, D, N_GROUPS), D ** -0.5),
        'router_b_grp': nrm(ks[19], (L, N_GROUPS), 0.01),
        'router_w_exp': nrm(ks[20], (L, D, N_EXPERTS), D ** -0.5),
        'router_b_exp': nrm(ks[21], (L, N_EXPERTS), 0.01),
        'expert_w1': nrm(ks[22], (L, N_EXPERTS, D, EXPERT_HIDDEN), D ** -0.5),
        'expert_w3': nrm(ks[23], (L, N_EXPERTS, D, EXPERT_HIDDEN), D ** -0.5),
        'expert_w2': nrm(ks[24], (L, N_EXPERTS, EXPERT_HIDDEN, D), EXPERT_HIDDEN ** -0.5),
    }


def reference(x, c, ada_w, ada_b, norm1_g, norm2_g, normf_g, w_in, rel_bias,
              cmp_pe_k, cmp_w1_k, cmp_w2_k, cmp_pe_v, cmp_w1_v, cmp_w2_v,
              w_branch_a, w_branch_b, w_out, router_w_grp, router_b_grp,
              router_w_exp, router_b_exp, expert_w1, expert_w3, expert_w2):
    B, S, D = x.shape
    G, HPG, dh = NSA_KV_GROUPS, NSA_HPG, HEAD_DIM
    offsets = []
    acc = 0
    for size in IN_SIZES[:-1]:
        acc += size
        offsets.append(acc)
    c_act = jax.nn.silu(c)
    for layer in range(DEPTH):
        mod = c_act @ ada_w[layer] + ada_b[layer]
        sh1, sc1, gt1, sh2, sc2, gt2 = jnp.split(mod, 6, axis=-1)
        xn = modulate(rmsnorm(x, norm1_g[layer]), sh1, sc1)
        proj = xn @ w_in[layer]
        (qa, ka, va, qb, kcp, vcp, ksl, vsl, ksw, vsw,
         g_nsa, g_a, g_b) = jnp.split(proj, offsets, axis=-1)
        heads = lambda t: t.reshape(B, S, SB_HEADS, dh).transpose(0, 2, 1, 3)
        o_a = stick_breaking_attention(heads(qa), heads(ka), heads(va))
        kvh = lambda t: t.reshape(B, S, G, dh).transpose(0, 2, 1, 3)
        q_nsa = qb.reshape(B, S, G, HPG, dh).transpose(0, 2, 3, 1, 4)
        k_c = compress_blocks(kvh(kcp), cmp_pe_k[layer], cmp_w1_k[layer], cmp_w2_k[layer])
        v_c = compress_blocks(kvh(vcp), cmp_pe_v[layer], cmp_w1_v[layer], cmp_w2_v[layer])
        gates = jax.nn.sigmoid(g_nsa).reshape(B, S, G, HPG, 3).transpose(0, 2, 3, 1, 4)
        o_b = nsa_attention(q_nsa, k_c, v_c, kvh(ksl), kvh(vsl), kvh(ksw), kvh(vsw),
                            gates, rel_bias)
        merged = (jax.nn.sigmoid(g_a) * (o_a @ w_branch_a[layer])
                  + jax.nn.sigmoid(g_b) * (o_b @ w_branch_b[layer]))
        x = x + gt1[:, None, :] * (merged @ w_out[layer])
        hn = modulate(rmsnorm(x, norm2_g[layer]), sh2, sc2)
        x = x + gt2[:, None, :] * hierarchical_moe(
            hn, router_w_grp[layer], router_b_grp[layer], router_w_exp[layer], router_b_exp[layer],
            expert_w1[layer], expert_w3[layer], expert_w2[layer])
    return rmsnorm(x, normf_g)
```

```python
import functools
import math

import numpy as np
import jax
import jax.numpy as jnp
from jax import lax
from jax.experimental import pallas as pl
from jax.experimental.pallas import tpu as pltpu

HEAD_DIM = 128
SB_HEADS = 8
NSA_HEADS = 8
NSA_KV_GROUPS = 2
NSA_HPG = NSA_HEADS // NSA_KV_GROUPS
CMP_LEN = 32
CMP_STRIDE = 16
SLC_BLOCK = 64
SLC_TOPN = 16
WINDOW = 512
REL_BUCKETS = 32
REL_MAX_DIST = 128
N_GROUPS = 8
EXPERTS_PER_GROUP = 8
N_EXPERTS = N_GROUPS * EXPERTS_PER_GROUP
TOPK_IN_GROUP = 2
RMS_EPS = 1e-6
FORCED_SCORE = 1e4

LANE = 128
SUBLANE = 8
VMEM_LIMIT_BYTES = 56 * 1024 * 1024

NEG = -1e30
QT = 128
MOE_TILE = 256

HS_QA, HS_KA, HS_VA, HS_QB = 0, 8, 16, 24
HS_KCP, HS_VCP, HS_KSL, HS_VSL, HS_KSW, HS_VSW = 32, 34, 36, 38, 40, 42
N_HEAD_SLOTS = 44

_F32 = jnp.float32
_BF16 = jnp.bfloat16


def _cparams(sem):
    return pltpu.CompilerParams(dimension_semantics=sem, vmem_limit_bytes=VMEM_LIMIT_BYTES)


def _dot(a, b):
    return jnp.dot(a, b, preferred_element_type=_F32)


def _dot_nt(a, b):
    return lax.dot_general(a, b, (((1,), (1,)), ((), ())), preferred_element_type=_F32)


def _split_hi_lo(x):
    hi = x.astype(_BF16)
    lo = (x - hi.astype(_F32)).astype(_BF16)
    return hi, lo


def _adaln_kernel(c_ref, w_ref, b_ref, o_ref):
    c = c_ref[...]
    ca = c * jax.nn.sigmoid(c)
    o_ref[...] = jnp.dot(ca, w_ref[...], preferred_element_type=_F32,
                         precision=lax.Precision.HIGHEST) + b_ref[...]


def _adaln(c, w, b):
    bsz, d = c.shape
    n = w.shape[1]
    tn = 1024
    cp = jnp.pad(c, ((0, SUBLANE - bsz), (0, 0)))
    out = pl.pallas_call(
        _adaln_kernel,
        grid=(n // tn,),
        in_specs=[pl.BlockSpec((SUBLANE, d), lambda j: (0, 0)),
                  pl.BlockSpec((d, tn), lambda j: (0, j)),
                  pl.BlockSpec((1, tn), lambda j: (0, j))],
        out_specs=pl.BlockSpec((SUBLANE, tn), lambda j: (0, j)),
        out_shape=jax.ShapeDtypeStruct((SUBLANE, n), _F32),
        compiler_params=_cparams(("arbitrary",)),
        name="adaln",
    )(cp, w, b.reshape(1, n))
    return out[:bsz]


def _nmm_kernel(x_ref, g_ref, sh_ref, sc_ref, w_ref, cs_ref, o_ref, xn_ref, *, heads_out, act):
    @pl.when(pl.program_id(2) == 0)
    def _():
        x = x_ref[0]
        ms = jnp.mean(x * x, axis=-1, keepdims=True)
        y = x * lax.rsqrt(ms + RMS_EPS) * g_ref[...]
        y = y * (1.0 + sc_ref[0]) + sh_ref[0]
        xn_ref[...] = y.astype(_BF16)

    acc = _dot(xn_ref[...], w_ref[...]) * cs_ref[...]
    if act == "sigmoid":
        acc = jax.nn.sigmoid(acc)
    if heads_out:
        for h in range(acc.shape[1] // HEAD_DIM):
            o_ref[0, h] = acc[:, h * HEAD_DIM:(h + 1) * HEAD_DIM].astype(o_ref.dtype)
    else:
        o_ref[0] = acc.astype(o_ref.dtype)


def _norm_mod_matmul(x, g, sh, sc, w, colscale, *, heads_out, act, out_dtype, tm, tn):
    bsz, s, d = x.shape
    n = w.shape[1]
    grid = (bsz, s // tm, n // tn)
    if heads_out:
        out_shape = jax.ShapeDtypeStruct((bsz, n // HEAD_DIM, s, HEAD_DIM), out_dtype)
        out_spec = pl.BlockSpec((1, tn // HEAD_DIM, tm, HEAD_DIM), lambda b, i, j: (b, j, i, 0))
    else:
        out_shape = jax.ShapeDtypeStruct((bsz, s, n), out_dtype)
        out_spec = pl.BlockSpec((1, tm, tn), lambda b, i, j: (b, i, j))
    return pl.pallas_call(
        functools.partial(_nmm_kernel, heads_out=heads_out, act=act),
        grid=grid,
        in_specs=[pl.BlockSpec((1, tm, d), lambda b, i, j: (b, i, 0)),
                  pl.BlockSpec((1, d), lambda b, i, j: (0, 0)),
                  pl.BlockSpec((1, 1, d), lambda b, i, j: (b, 0, 0)),
                  pl.BlockSpec((1, 1, d), lambda b, i, j: (b, 0, 0)),
                  pl.BlockSpec((d, tn), lambda b, i, j: (0, j)),
                  pl.BlockSpec((1, tn), lambda b, i, j: (0, j))],
        out_specs=out_spec,
        out_shape=out_shape,
        scratch_shapes=[pltpu.VMEM((tm, d), _BF16)],
        compiler_params=_cparams(("parallel", "parallel", "arbitrary")),
        name="norm_mod_matmul",
    )(x, g.reshape(1, d), sh.reshape(bsz, 1, d), sc.reshape(bsz, 1, d), w, colscale.reshape(1, n))


def _cmp_kernel(kv_ref, w1c_ref, pe_ref, w1_ref, w2_ref, o_ref):
    kv = kv_ref[0, 0]
    nc = kv.shape[0]
    hdim = w2_ref.shape[0]
    h = _dot(kv, w1c_ref[...])
    c = jnp.dot(pe_ref[...], w1_ref[...], preferred_element_type=_F32,
                precision=lax.Precision.HIGHEST)
    u = h[:, :hdim]
    v = pltpu.roll(h[:, hdim:], nc - 1, 0)
    hid = jax.nn.gelu(u + v + c[0:1])
    out = _dot(hid.astype(_BF16), w2_ref[...])
    row = lax.broadcasted_iota(jnp.int32, out.shape, 0)
    o_ref[0, 0] = jnp.where(row < nc - 1, out, 0.0)


def _compress(kv_chunks, pe, w1, w2):
    bsz, g, nc, ck = kv_chunks.shape
    hdim = w1.shape[1]
    dh = w2.shape[1]
    w1c = jnp.concatenate([w1[:ck], w1[ck:]], axis=1).astype(_BF16)
    pe_p = jnp.pad(pe.reshape(1, -1), ((0, SUBLANE - 1), (0, 0)))
    return pl.pallas_call(
        _cmp_kernel,
        grid=(bsz, g),
        in_specs=[pl.BlockSpec((1, 1, nc, ck), lambda b, gi: (b, gi, 0, 0)),
                  pl.BlockSpec((ck, 2 * hdim), lambda b, gi: (0, 0)),
                  pl.BlockSpec((SUBLANE, 2 * ck), lambda b, gi: (0, 0)),
                  pl.BlockSpec((2 * ck, hdim), lambda b, gi: (0, 0)),
                  pl.BlockSpec((hdim, dh), lambda b, gi: (0, 0))],
        out_specs=pl.BlockSpec((1, 1, nc, dh), lambda b, gi: (b, gi, 0, 0)),
        out_shape=jax.ShapeDtypeStruct((bsz, g, nc, dh), _F32),
        compiler_params=_cparams(("parallel", "parallel")),
        name="compress_blocks",
    )(kv_chunks, w1c, pe_p, w1, w2.astype(_BF16))


SB_TQ = 512
SB_TK = 512


def _sb_kernel(q_ref, k_ref, v_ref, tri_ref, o_ref):
    i = pl.program_id(2)
    tq = q_ref.shape[2]
    q = q_ref[0, 0]
    tri = tri_ref[...]
    t_pos = i * tq + lax.broadcasted_iota(jnp.int32, (tq, LANE), 0)
    lane = lax.broadcasted_iota(jnp.int32, (tq, LANE), 1)
    n_sub = SB_TK // LANE

    def tile(it, carry):
        r_run, acc = carry
        j = i - it
        for c in reversed(range(n_sub)):
            start = pl.multiple_of(j * SB_TK + c * LANE, LANE)
            kc = k_ref[0, 0, pl.ds(start, LANE), :]
            vc = v_ref[0, 0, pl.ds(start, LANE), :]
            z = _dot_nt(q, kc)
            causal = (start + lane) < t_pos
            sp = jnp.maximum(z, 0.0) + jnp.log(1.0 + jnp.exp(-jnp.abs(z)))
            l1m = jnp.where(causal, -sp, 0.0)
            hi, lo = _split_hi_lo(l1m)
            w = _dot(hi, tri) + _dot(lo, tri)
            after = r_run + w[:, :LANE]
            a = jnp.where(causal, jnp.exp(z - sp + after), 0.0)
            acc = acc + _dot(a.astype(_BF16), vc)
            r_run = r_run + w[:, LANE:]
        return r_run, acc

    init = (jnp.zeros((tq, LANE), _F32), jnp.zeros((tq, HEAD_DIM), _F32))
    _, acc = lax.fori_loop(0, i + 1, tile, init)
    o_ref[0] = acc.astype(o_ref.dtype)


def _stick_breaking(heads, bsz, s):
    tq = min(SB_TQ, s)
    assert s % SB_TK == 0 and tq == SB_TK
    jj = np.arange(LANE)
    tri = np.concatenate([(jj[:, None] > jj[None, :]).astype(np.float32),
                          np.ones((LANE, LANE), np.float32)], axis=1)
    return pl.pallas_call(
        _sb_kernel,
        grid=(bsz, SB_HEADS, s // tq),
        in_specs=[pl.BlockSpec((1, 1, tq, HEAD_DIM), lambda b, h, i: (b, HS_QA + h, i, 0)),
                  pl.BlockSpec((1, 1, s, HEAD_DIM), lambda b, h, i: (b, HS_KA + h, 0, 0)),
                  pl.BlockSpec((1, 1, s, HEAD_DIM), lambda b, h, i: (b, HS_VA + h, 0, 0)),
                  pl.BlockSpec((LANE, 2 * LANE), lambda b, h, i: (0, 0))],
        out_specs=pl.BlockSpec((1, tq, HEAD_DIM), lambda b, h, i: (b, i, h)),
        out_shape=jax.ShapeDtypeStruct((bsz, s, SB_HEADS * HEAD_DIM), _BF16),
        compiler_params=_cparams(("parallel", "parallel", "arbitrary")),
        name="stick_breaking_attention",
    )(heads, heads, heads, jnp.asarray(tri, _BF16))


NSA_FAR_TK = 512
NEAR_CMP = 16
CMP_PAD = 8


def _softmax_parts(parts, masks):
    m = None
    for sc in parts:
        mm = jnp.max(sc, axis=-1, keepdims=True)
        m = mm if m is None else jnp.maximum(m, mm)
    es = [jnp.where(mk, jnp.exp(sc - m), 0.0) for sc, mk in zip(parts, masks)]
    l = None
    for e in es:
        ll = jnp.sum(e, axis=-1, keepdims=True)
        l = ll if l is None else l + ll
    inv = 1.0 / jnp.maximum(l, 1e-30)
    return [e * inv for e in es]


def _nsa_kernel(q_ref, kc_ref, vc_ref, ksl_ref, vsl_ref, ksw_ref, vsw_ref, gate_ref,
                bw_ref, bc_ref, ov_ref, exp_ref, o_ref, *, n_sel):
    i = pl.program_id(2)
    hpg = q_ref.shape[1]
    rows = hpg * QT
    s_len = ksl_ref.shape[2]
    q = q_ref[0].reshape(rows, HEAD_DIM)

    def scores(k):
        return _dot_nt(q, k).reshape(hpg, QT, k.shape[0])

    def pv(p, v):
        return _dot(p.reshape(rows, p.shape[-1]).astype(_BF16), v)

    r_io = lax.broadcasted_iota(jnp.int32, (QT, LANE), 0)
    c_io = lax.broadcasted_iota(jnp.int32, (QT, LANE), 1)

    ncp = kc_ref.shape[2] - NEAR_CMP
    kc_far = kc_ref[0, 0, pl.ds(0, ncp), :].astype(_BF16)
    vc_far = vc_ref[0, 0, pl.ds(0, ncp), :].astype(_BF16)
    near0 = pl.multiple_of(i * (QT // CMP_STRIDE), SUBLANE)
    kc_near = kc_ref[0, 0, pl.ds(near0, NEAR_CMP), :].astype(_BF16)
    vc_near = vc_ref[0, 0, pl.ds(near0, NEAR_CMP), :].astype(_BF16)
    p_io = lax.broadcasted_iota(jnp.int32, (QT, ncp), 1)
    far_mask = ((p_io >= CMP_PAD) & (p_io < near0))[None]
    u_io = lax.broadcasted_iota(jnp.int32, (QT, NEAR_CMP), 1)
    ru_io = lax.broadcasted_iota(jnp.int32, (QT, NEAR_CMP), 0)
    d_near = ru_io - CMP_STRIDE * u_io + (CMP_STRIDE * CMP_PAD - CMP_LEN + 1)
    near_mask = ((d_near >= 0) & (near0 + u_io >= CMP_PAD))[None]
    s_far = jnp.where(far_mask, scores(kc_far), NEG)
    s_near = jnp.where(near_mask, scores(kc_near) + bc_ref[...], NEG)
    p_far, p_near = _softmax_parts([s_far, s_near], [far_mask, near_mask])
    o_c = pv(p_far, vc_far) + pv(p_near, vc_near)

    ps_far = jnp.sum(p_far, axis=0)
    ps_near = jnp.sum(p_near, axis=0)
    ov_far = ov_ref[pl.ds(0, ncp), :].astype(_BF16)
    ov_near = ov_ref[pl.ds(near0, NEAR_CMP), :].astype(_BF16)
    fh, fl = _split_hi_lo(ps_far)
    nh, nl = _split_hi_lo(ps_near)
    imp = _dot(fh, ov_far) + _dot(fl, ov_far) + _dot(nh, ov_near) + _dot(nl, ov_near)

    t_abs = i * QT + r_io
    cur = t_abs >> int(math.log2(SLC_BLOCK))
    forced = (c_io == 0) | (c_io == cur) | (c_io == cur - 1)
    valid_b = (c_io * SLC_BLOCK) <= t_abs
    work = jnp.where(forced, FORCED_SCORE, jnp.where(valid_b, imp, -1.0))
    lane_f = c_io.astype(_F32)
    sel = jnp.zeros((QT, LANE), _F32)
    for _ in range(n_sel):
        mx = jnp.max(work, axis=-1, keepdims=True)
        first = jnp.min(jnp.where(work == mx, lane_f, 1e9), axis=-1, keepdims=True)
        pick = lane_f == first
        sel = jnp.where(pick, 1.0, sel)
        work = jnp.where(pick, -2.0, work)

    n_w = WINDOW // QT + 1
    sw_parts, sw_v = [], []
    for m in range(n_w):
        start = pl.multiple_of(jnp.maximum(i - m, 0) * QT, QT)
        kw = ksw_ref[0, 0, pl.ds(start, QT), :]
        sw_v.append(vsw_ref[0, 0, pl.ds(start, QT), :])
        sc = scores(kw)
        if m < 2:
            sc = sc + bw_ref[m]
        if m == 0:
            sc = jnp.where((c_io <= r_io)[None], sc, NEG)
        elif m == n_w - 1:
            sc = jnp.where((c_io > r_io)[None], sc, NEG)
        if m > 0:
            sc = sc + jnp.where(i >= m, 0.0, NEG)
        sw_parts.append(sc)
    m_w = functools.reduce(jnp.maximum, [jnp.max(sc, axis=-1, keepdims=True) for sc in sw_parts])
    e_w = [jnp.exp(sc - m_w) for sc in sw_parts]
    l_w = functools.reduce(lambda a, b: a + b, [jnp.sum(e, axis=-1, keepdims=True) for e in e_w])
    o_w = functools.reduce(lambda a, b: a + b, [pv(e, v) for e, v in zip(e_w, sw_v)])
    o_w = o_w * (1.0 / jnp.maximum(l_w, 1e-30)).reshape(rows, 1)

    blk_per_q = QT // SLC_BLOCK
    sel_b = sel.astype(_BF16)
    e_row = lax.broadcasted_iota(jnp.int32, (LANE, 2 * QT), 0)
    e_col = lax.broadcasted_iota(jnp.int32, (LANE, 2 * QT), 1)
    e_near = (e_row == (i - 1) * blk_per_q + (e_col >> int(math.log2(SLC_BLOCK)))).astype(_BF16)
    hit_near = _dot(sel_b, e_near)
    st_prev = pl.multiple_of(jnp.maximum(i - 1, 0) * QT, QT)
    st_cur = pl.multiple_of(i * QT, QT)
    s_prev = scores(ksl_ref[0, 0, pl.ds(st_prev, QT), :]) + bw_ref[1]
    s_cur = scores(ksl_ref[0, 0, pl.ds(st_cur, QT), :]) + bw_ref[0]
    s_prev = jnp.where((hit_near[:, :QT] > 0.5)[None], s_prev, NEG)
    s_cur = jnp.where(((hit_near[:, QT:] > 0.5) & (c_io <= r_io))[None], s_cur, NEG)
    m_s = jnp.maximum(jnp.max(s_prev, axis=-1, keepdims=True), jnp.max(s_cur, axis=-1, keepdims=True))
    e_prev = jnp.exp(s_prev - m_s)
    e_cur = jnp.exp(s_cur - m_s)
    l_s = jnp.sum(e_prev, axis=-1, keepdims=True) + jnp.sum(e_cur, axis=-1, keepdims=True)
    acc_s = (pv(e_prev, vsl_ref[0, 0, pl.ds(st_prev, QT), :])
             + pv(e_cur, vsl_ref[0, 0, pl.ds(st_cur, QT), :]))

    sel_far = jnp.where(c_io < (i - 1) * blk_per_q, sel, 0.0).astype(_BF16)
    n_far = ((i - 1) * QT + NSA_FAR_TK - 1) // NSA_FAR_TK
    n_far = jnp.maximum(n_far, 0)

    def far_tile(kt, carry):
        m_run, l_run, acc = carry
        start = pl.multiple_of(kt * NSA_FAR_TK, NSA_FAR_TK)
        msk = (_dot(sel_far, exp_ref[kt]) > 0.5)[None]
        sc = jnp.where(msk, scores(ksl_ref[0, 0, pl.ds(start, NSA_FAR_TK), :]), NEG)
        m_new = jnp.maximum(m_run, jnp.max(sc, axis=-1, keepdims=True))
        alpha = jnp.exp(m_run - m_new)
        e = jnp.exp(sc - m_new)
        l_new = alpha * l_run + jnp.sum(e, axis=-1, keepdims=True)
        acc = acc * alpha.reshape(rows, 1) + pv(e, vsl_ref[0, 0, pl.ds(start, NSA_FAR_TK), :])
        return m_new, l_new, acc

    if s_len >= NSA_FAR_TK:
        m_s, l_s, acc_s = lax.fori_loop(0, n_far, far_tile, (m_s, l_s, acc_s))
    o_s = acc_s * (1.0 / jnp.maximum(l_s, 1e-30)).reshape(rows, 1)

    gt = gate_ref[0]
    for h in range(hpg):
        sl = slice(h * QT, (h + 1) * QT)
        out_h = (gt[:, 3 * h:3 * h + 1] * o_c[sl] + gt[:, 3 * h + 1:3 * h + 2] * o_s[sl]
                 + gt[:, 3 * h + 2:3 * h + 3] * o_w[sl])
        o_ref[0, :, h * HEAD_DIM:(h + 1) * HEAD_DIM] = out_h.astype(o_ref.dtype)


def _t5_bucket_np(dist):
    n = np.maximum(dist, 0)
    max_exact = REL_BUCKETS // 2
    nf = np.maximum(n, 1).astype(np.float64)
    large = max_exact + (np.log(nf / max_exact) / math.log(REL_MAX_DIST / max_exact)
                         * (REL_BUCKETS - max_exact)).astype(np.int64)
    large = np.minimum(large, REL_BUCKETS - 1)
    return np.where(n < max_exact, n, large)


def _bias_tables(rel_bias):
    tbl = rel_bias.astype(_F32)
    delta = tbl - tbl[REL_BUCKETS - 1][None, :]
    r = np.arange(QT)[:, None]
    c = np.arange(QT)[None, :]
    bw = []
    for m in range(2):
        d = r - c + QT * m
        tab = jnp.where(jnp.asarray(d >= 0)[:, :, None], delta[_t5_bucket_np(d)], 0.0)
        bw.append(tab.transpose(2, 0, 1))
    bw = jnp.stack(bw)
    u = np.arange(NEAR_CMP)[None, :]
    dc = r - CMP_STRIDE * u + (CMP_STRIDE * CMP_PAD - CMP_LEN + 1)
    bc = jnp.where(jnp.asarray(dc >= 0)[:, :, None], delta[_t5_bucket_np(dc)], 0.0).transpose(2, 0, 1)
    return bw, bc


def _nsa(heads, k_cmp, v_cmp, gates, rel_bias, bsz, s):
    g, hpg = NSA_KV_GROUPS, NSA_HPG
    nc = s // CMP_STRIDE
    n_slc = s // SLC_BLOCK
    assert n_slc <= LANE and s % QT == 0
    n_sel = min(SLC_TOPN, n_slc)
    pad = ((0, 0), (0, 0), (CMP_PAD, NEAR_CMP - CMP_PAD), (0, 0))
    kcp = jnp.pad(k_cmp, pad)
    vcp = jnp.pad(v_cmp, pad)
    ncp_rows = nc + NEAR_CMP
    bw, bc = _bias_tables(rel_bias)
    n_idx = np.arange(nc)[:, None] * CMP_STRIDE
    j_idx = np.arange(LANE)[None, :] * SLC_BLOCK
    ov = np.clip(np.minimum(n_idx + CMP_LEN, j_idx + SLC_BLOCK) - np.maximum(n_idx, j_idx), 0, None) / CMP_LEN
    ov[nc - 1:] = 0.0
    ov[:, n_slc:] = 0.0
    ov_p = np.zeros((ncp_rows, LANE), np.float32)
    ov_p[CMP_PAD:CMP_PAD + nc] = ov
    n_ft = max(s // NSA_FAR_TK, 1)
    eb = np.arange(LANE)[None, :, None]
    ec = np.arange(NSA_FAR_TK)[None, None, :]
    et = np.arange(n_ft)[:, None, None]
    expand = (eb == (et * NSA_FAR_TK + ec) // SLC_BLOCK).astype(np.float32)

    kv_spec = lambda slot: pl.BlockSpec((1, 1, s, HEAD_DIM), lambda b, gi, i: (b, slot + gi, 0, 0))
    return pl.pallas_call(
        functools.partial(_nsa_kernel, n_sel=n_sel),
        grid=(bsz, g, s // QT),
        in_specs=[pl.BlockSpec((1, hpg, QT, HEAD_DIM), lambda b, gi, i: (b, HS_QB // hpg + gi, i, 0)),
                  pl.BlockSpec((1, 1, ncp_rows, HEAD_DIM), lambda b, gi, i: (b, gi, 0, 0)),
                  pl.BlockSpec((1, 1, ncp_rows, HEAD_DIM), lambda b, gi, i: (b, gi, 0, 0)),
                  kv_spec(HS_KSL), kv_spec(HS_VSL), kv_spec(HS_KSW), kv_spec(HS_VSW),
                  pl.BlockSpec((1, QT, LANE), lambda b, gi, i: (b, i, gi)),
                  pl.BlockSpec((2, hpg, QT, QT), lambda b, gi, i: (0, gi, 0, 0)),
                  pl.BlockSpec((hpg, QT, NEAR_CMP), lambda b, gi, i: (gi, 0, 0)),
                  pl.BlockSpec((ncp_rows, LANE), lambda b, gi, i: (0, 0)),
                  pl.BlockSpec((n_ft, LANE, NSA_FAR_TK), lambda b, gi, i: (0, 0, 0))],
        out_specs=pl.BlockSpec((1, QT, hpg * HEAD_DIM), lambda b, gi, i: (b, i, gi)),
        out_shape=jax.ShapeDtypeStruct((bsz, s, NSA_HEADS * HEAD_DIM), _BF16),
        compiler_params=_cparams(("parallel", "parallel", "arbitrary")),
        name="nsa_attention",
    )(heads, kcp, vcp, heads, heads, heads, heads, gates, bw, bc,
      jnp.asarray(ov_p), jnp.asarray(expand, _BF16))


def _merge_kernel(oa_ref, ob_ref, wa_ref, wb_ref, ga_ref, gb_ref, o_ref):
    a = _dot(oa_ref[...], wa_ref[...])
    b = _dot(ob_ref[...], wb_ref[...])
    o_ref[...] = (ga_ref[...].astype(_F32) * a + gb_ref[...].astype(_F32) * b).astype(o_ref.dtype)


def _merge(o_a, o_b, w_a, w_b, gates_ab, tm=512, tn=512):
    t, ka = o_a.shape
    kb = o_b.shape[1]
    d = w_a.shape[1]
    nj = d // tn
    return pl.pallas_call(
        _merge_kernel,
        grid=(t // tm, nj),
        in_specs=[pl.BlockSpec((tm, ka), lambda i, j: (i, 0)),
                  pl.BlockSpec((tm, kb), lambda i, j: (i, 0)),
                  pl.BlockSpec((ka, tn), lambda i, j: (0, j)),
                  pl.BlockSpec((kb, tn), lambda i, j: (0, j)),
                  pl.BlockSpec((tm, tn), lambda i, j: (i, j)),
                  pl.BlockSpec((tm, tn), lambda i, j: (i, j + nj))],
        out_specs=pl.BlockSpec((tm, tn), lambda i, j: (i, j)),
        out_shape=jax.ShapeDtypeStruct((t, d), _BF16),
        compiler_params=_cparams(("parallel", "arbitrary")),
        name="branch_merge",
    )(o_a, o_b, w_a, w_b, gates_ab, gates_ab)


def _outproj_kernel(m_ref, w_ref, x_ref, gt_ref, o_ref):
    o_ref[0] = x_ref[0] + gt_ref[0] * _dot(m_ref[0], w_ref[...])


def _out_proj(merged, w, x, gt, tm=512, tn=512):
    bsz, s, d = x.shape
    k = merged.shape[-1]
    return pl.pallas_call(
        _outproj_kernel,
        grid=(bsz, s // tm, d // tn),
        in_specs=[pl.BlockSpec((1, tm, k), lambda b, i, j: (b, i, 0)),
                  pl.BlockSpec((k, tn), lambda b, i, j: (0, j)),
                  pl.BlockSpec((1, tm, tn), lambda b, i, j: (b, i, j)),
                  pl.BlockSpec((1, 1, tn), lambda b, i, j: (b, 0, j))],
        out_specs=pl.BlockSpec((1, tm, tn), lambda b, i, j: (b, i, j)),
        out_shape=jax.ShapeDtypeStruct((bsz, s, d), _F32),
        compiler_params=_cparams(("parallel", "parallel", "arbitrary")),
        name="out_proj_residual",
    )(merged.reshape(bsz, s, k), w, x, gt.reshape(bsz, 1, d))


def _router_kernel(x_ref, g_ref, sh_ref, sc_ref, w_ref, b_ref, hn_ref, e_ref, wt_ref):
    x = x_ref[0]
    ms = jnp.mean(x * x, axis=-1, keepdims=True)
    hn = x * lax.rsqrt(ms + RMS_EPS) * g_ref[...]
    hn = hn * (1.0 + sc_ref[0]) + sh_ref[0]
    hn_ref[0] = hn
    logits = jnp.dot(hn, w_ref[...], preferred_element_type=_F32,
                     precision=lax.Precision.HIGHEST) + b_ref[...]
    lane = lax.broadcasted_iota(jnp.int32, logits.shape, 1)
    lane_f = lane.astype(_F32)

    def first_max(vals):
        mx = jnp.max(vals, axis=-1, keepdims=True)
        idx = jnp.min(jnp.where(vals == mx, lane_f, 1e9), axis=-1, keepdims=True)
        return mx, idx

    is_grp = lane < N_GROUPS
    lg = jnp.where(is_grp, logits, NEG)
    eg = jnp.where(is_grp, jnp.exp(lg - jnp.max(lg, axis=-1, keepdims=True)), 0.0)
    pg = eg / jnp.sum(eg, axis=-1, keepdims=True)
    pg_sel, g_sel = first_max(jnp.where(is_grp, pg, -1.0))
    lo_lane = N_GROUPS + EXPERTS_PER_GROUP * g_sel
    in_grp = (lane_f >= lo_lane) & (lane_f < lo_lane + EXPERTS_PER_GROUP)
    le = jnp.where(in_grp, logits, NEG)
    ee = jnp.where(in_grp, jnp.exp(le - jnp.max(le, axis=-1, keepdims=True)), 0.0)
    pe = ee / jnp.sum(ee, axis=-1, keepdims=True)
    cand = jnp.where(in_grp, pe, -1.0)
    w0, i0 = first_max(cand)
    w1, i1 = first_max(jnp.where(lane_f == i0, -1.0, cand))
    den = w0 + w1
    w0n = w0 / den * pg_sel
    w1n = w1 / den * pg_sel
    e_ref[0] = jnp.where(lane == 0, i0 - N_GROUPS, jnp.where(lane == 1, i1 - N_GROUPS, 0.0)).astype(jnp.int32)
    wt_ref[0] = jnp.where(lane == 0, w0n, jnp.where(lane == 1, w1n, 0.0))


def _router(x1, g, sh, sc, w_grp, b_grp, w_exp, b_exp, tm=256):
    bsz, s, d = x1.shape
    nl = N_GROUPS + N_EXPERTS
    wr = jnp.pad(jnp.concatenate([w_grp, w_exp], axis=1), ((0, 0), (0, LANE - nl)))
    br = jnp.pad(jnp.concatenate([b_grp, b_exp]), (0, LANE - nl)).reshape(1, LANE)
    spec_t = lambda w: pl.BlockSpec((1, tm, w), lambda b, i: (b, i, 0))
    return pl.pallas_call(
        _router_kernel,
        grid=(bsz, s // tm),
        in_specs=[spec_t(d),
                  pl.BlockSpec((1, d), lambda b, i: (0, 0)),
                  pl.BlockSpec((1, 1, d), lambda b, i: (b, 0, 0)),
                  pl.BlockSpec((1, 1, d), lambda b, i: (b, 0, 0)),
                  pl.BlockSpec((d, LANE), lambda b, i: (0, 0)),
                  pl.BlockSpec((1, LANE), lambda b, i: (0, 0))],
        out_specs=[spec_t(d), spec_t(LANE), spec_t(LANE)],
        out_shape=[jax.ShapeDtypeStruct((bsz, s, d), _F32),
                   jax.ShapeDtypeStruct((bsz, s, LANE), jnp.int32),
                   jax.ShapeDtypeStruct((bsz, s, LANE), _F32)],
        compiler_params=_cparams(("parallel", "arbitrary")),
        name="norm_mod_router",
    )(x1, g.reshape(1, d), sh.reshape(bsz, 1, d), sc.reshape(bsz, 1, d), wr, br)


def _row_copy(src_hbm, dst_vmem, src_row, dst_row, sem):
    return pltpu.make_async_copy(src_hbm.at[pl.ds(src_row, 1)], dst_vmem.at[pl.ds(dst_row, 1)], sem)


def _expert_kernel(blk_e_ref, tok_ref, hn_hbm, rw_ref, w1_ref, w3_ref, w2_ref, y_ref, xs_ref, sem):
    del blk_e_ref
    n_rows = xs_ref.shape[0]

    def start(r, c):
        _row_copy(hn_hbm, xs_ref, tok_ref[0, 0, r], r, sem).start()
        return c

    lax.fori_loop(0, n_rows, start, 0)

    def wait(r, c):
        _row_copy(hn_hbm, xs_ref, 0, r, sem).wait()
        return c

    lax.fori_loop(0, n_rows, wait, 0)
    xs = xs_ref[...].astype(_BF16)
    h1 = _dot(xs, w1_ref[0])
    h3 = _dot(xs, w3_ref[0])
    hid = (h1 * jax.nn.sigmoid(h1)) * h3
    y = _dot(hid.astype(_BF16), w2_ref[0])
    y_ref[...] = y * rw_ref[...]


def _experts(hn2d, blk_e, row_tok, row_w, w1, w3, w2):
    t, d = hn2d.shape
    n_rows = row_tok.shape[0]
    n_blk = n_rows // MOE_TILE
    hid = w1.shape[-1]
    grid_spec = pltpu.PrefetchScalarGridSpec(
        num_scalar_prefetch=1,
        grid=(n_blk,),
        in_specs=[pl.BlockSpec((1, 1, MOE_TILE), lambda i, be: (i, 0, 0), memory_space=pltpu.SMEM),
                  pl.BlockSpec(memory_space=pl.ANY),
                  pl.BlockSpec((MOE_TILE, 1), lambda i, be: (i, 0)),
                  pl.BlockSpec((1, d, hid), lambda i, be: (be[i], 0, 0)),
                  pl.BlockSpec((1, d, hid), lambda i, be: (be[i], 0, 0)),
                  pl.BlockSpec((1, hid, d), lambda i, be: (be[i], 0, 0))],
        out_specs=pl.BlockSpec((MOE_TILE, d), lambda i, be: (i, 0)),
        scratch_shapes=[pltpu.VMEM((MOE_TILE, d), _F32), pltpu.SemaphoreType.DMA(())],
    )
    return pl.pallas_call(
        _expert_kernel,
        grid_spec=grid_spec,
        out_shape=jax.ShapeDtypeStruct((n_rows, d), _F32),
        compiler_params=_cparams(("arbitrary",)),
        name="expert_mlp",
    )(blk_e, row_tok.reshape(n_blk, 1, MOE_TILE), hn2d, row_w.reshape(n_rows, 1), w1, w3, w2)


CMB_TM = 256


def _combine_kernel(pos_ref, y_hbm, x_ref, gt_ref, g_ref, o_ref, buf0, buf1, sem):
    tm = buf0.shape[0]

    def start(r, c):
        _row_copy(y_hbm, buf0, pos_ref[0, 0, 2 * r], r, sem).start()
        _row_copy(y_hbm, buf1, pos_ref[0, 0, 2 * r + 1], r, sem).start()
        return c

    lax.fori_loop(0, tm, start, 0)

    def wait(r, c):
        _row_copy(y_hbm, buf0, 0, r, sem).wait()
        _row_copy(y_hbm, buf1, 0, r, sem).wait()
        return c

    lax.fori_loop(0, tm, wait, 0)
    x2 = x_ref[0] + gt_ref[0] * (buf0[...] + buf1[...])
    ms = jnp.mean(x2 * x2, axis=-1, keepdims=True)
    o_ref[0] = x2 * lax.rsqrt(ms + RMS_EPS) * g_ref[...]


def _combine(y, pos, x1, gt, g):
    bsz, s, d = x1.shape
    tm = CMB_TM
    n_t = (bsz * s) // tm
    per_b = s // tm
    return pl.pallas_call(
        _combine_kernel,
        grid=(n_t,),
        in_specs=[pl.BlockSpec((1, 1, 2 * tm), lambda i: (i, 0, 0), memory_space=pltpu.SMEM),
                  pl.BlockSpec(memory_space=pl.ANY),
                  pl.BlockSpec((1, tm, d), lambda i: (i // per_b, i % per_b, 0)),
                  pl.BlockSpec((1, 1, d), lambda i: (i // per_b, 0, 0)),
                  pl.BlockSpec((1, d), lambda i: (0, 0))],
        out_specs=pl.BlockSpec((1, tm, d), lambda i: (i // per_b, i % per_b, 0)),
        out_shape=jax.ShapeDtypeStruct((bsz, s, d), _F32),
        scratch_shapes=[pltpu.VMEM((tm, d), _F32), pltpu.VMEM((tm, d), _F32), pltpu.SemaphoreType.DMA(())],
        compiler_params=_cparams(("arbitrary",)),
        name="moe_combine_final_norm",
    )(pos.reshape(n_t, 1, 2 * tm), y, x1, gt.reshape(bsz, 1, d), g.reshape(1, d))


def _dispatch_tables(e_id, w_top, t):
    n = t * TOPK_IN_GROUP
    e_flat = e_id.reshape(n)
    w_flat = w_top.reshape(n)
    tok = jnp.repeat(jnp.arange(t, dtype=jnp.int32), TOPK_IN_GROUP)
    order = jnp.argsort(e_flat)
    e_sorted = e_flat[order]
    counts = jnp.bincount(e_flat, length=N_EXPERTS)
    start = jnp.cumsum(counts) - counts
    padded = (counts + MOE_TILE - 1) // MOE_TILE * MOE_TILE
    pend = jnp.cumsum(padded)
    pstart = pend - padded
    dest = (pstart[e_sorted] + (jnp.arange(n) - start[e_sorted])).astype(jnp.int32)
    n_rows = (n + N_EXPERTS * (MOE_TILE - 1) + MOE_TILE - 1) // MOE_TILE * MOE_TILE
    n_blk = n_rows // MOE_TILE
    row_tok = jnp.zeros((n_rows,), jnp.int32).at[dest].set(tok[order])
    row_w = jnp.zeros((n_rows,), _F32).at[dest].set(w_flat[order])
    blk_e = jnp.minimum(jnp.searchsorted(pend, jnp.arange(n_blk) * MOE_TILE, side="right"),
                        N_EXPERTS - 1).astype(jnp.int32)
    pos = jnp.zeros((n,), jnp.int32).at[order].set(dest)
    return blk_e, row_tok, row_w, pos


def kernel(x, c, ada_w, ada_b, norm1_g, norm2_g, normf_g, w_in, rel_bias, cmp_pe_k, cmp_w1_k, cmp_w2_k,
           cmp_pe_v, cmp_w1_v, cmp_w2_v, w_branch_a, w_branch_b, w_out, router_w_grp, router_b_grp,
           router_w_exp, router_b_exp, expert_w1, expert_w3, expert_w2):
    bsz, s, d = x.shape
    depth = ada_w.shape[0]
    assert depth == 1, "the final norm is fused into the MoE combine of a single layer"
    dh = HEAD_DIM
    n_hcols = N_HEAD_SLOTS * dh
    n_gate3 = 3 * NSA_HEADS
    per_g = n_gate3 // NSA_KV_GROUPS
    scale = dh ** -0.5
    tm = min(512, s)
    for layer in range(depth):
        mod = _adaln(c, ada_w[layer], ada_b[layer])
        sh1, sc1, gt1, sh2, sc2, gt2 = jnp.split(mod, 6, axis=-1)
        wl = w_in[layer]
        cs = np.ones((n_hcols,), np.float32)
        cs[HS_QA * dh:(HS_QA + SB_HEADS) * dh] = scale
        cs[HS_QB * dh:(HS_QB + NSA_HEADS) * dh] = scale
        heads = _norm_mod_matmul(x, norm1_g[layer], sh1, sc1, wl[:, :n_hcols].astype(_BF16), jnp.asarray(cs),
                                 heads_out=True, act=None, out_dtype=_BF16, tm=tm, tn=512)
        wg = jnp.zeros((d, NSA_KV_GROUPS * LANE), w_in.dtype)
        for gi in range(NSA_KV_GROUPS):
            wg = wg.at[:, gi * LANE:gi * LANE + per_g].set(
                wl[:, n_hcols + gi * per_g:n_hcols + (gi + 1) * per_g])
        gates3 = _norm_mod_matmul(x, norm1_g[layer], sh1, sc1, wg.astype(_BF16),
                                  jnp.ones((NSA_KV_GROUPS * LANE,), _F32),
                                  heads_out=False, act="sigmoid", out_dtype=_F32, tm=tm, tn=NSA_KV_GROUPS * LANE)
        gates_ab = _norm_mod_matmul(x, norm1_g[layer], sh1, sc1, wl[:, n_hcols + n_gate3:].astype(_BF16),
                                    jnp.ones((2 * d,), _F32),
                                    heads_out=False, act="sigmoid", out_dtype=_BF16, tm=tm, tn=512)

        def chunks(slot):
            return heads[:, slot:slot + NSA_KV_GROUPS].reshape(bsz, NSA_KV_GROUPS, s // CMP_STRIDE, CMP_STRIDE * dh)

        k_cmp = _compress(chunks(HS_KCP), cmp_pe_k[layer], cmp_w1_k[layer], cmp_w2_k[layer])
        v_cmp = _compress(chunks(HS_VCP), cmp_pe_v[layer], cmp_w1_v[layer], cmp_w2_v[layer])
        o_a = _stick_breaking(heads, bsz, s)
        o_b = _nsa(heads, k_cmp, v_cmp, gates3, rel_bias, bsz, s)
        merged = _merge(o_a.reshape(bsz * s, -1), o_b.reshape(bsz * s, -1),
                        w_branch_a[layer].astype(_BF16), w_branch_b[layer].astype(_BF16),
                        gates_ab.reshape(bsz * s, 2 * d), tm=tm)
        x = _out_proj(merged, w_out[layer].astype(_BF16), x, gt1, tm=tm)
        hn, e_l, w_l = _router(x, norm2_g[layer], sh2, sc2, router_w_grp[layer], router_b_grp[layer],
                               router_w_exp[layer], router_b_exp[layer], tm=min(256, s))
        t = bsz * s
        blk_e, row_tok, row_w, pos = _dispatch_tables(e_l.reshape(t, LANE)[:, :TOPK_IN_GROUP],
                                                      w_l.reshape(t, LANE)[:, :TOPK_IN_GROUP], t)
        y = _experts(hn.reshape(t, d), blk_e, row_tok, row_w, expert_w1[layer].astype(_BF16),
                     expert_w3[layer].astype(_BF16), expert_w2[layer].astype(_BF16))
        x = _combine(y, pos, x, gt2, normf_g)
    return x
```

```python
import functools
import math

import numpy as np
import jax
import jax.numpy as jnp
from jax import lax
from jax.experimental import pallas as pl
from jax.experimental.pallas import tpu as pltpu

HEAD_DIM = 128
SB_HEADS = 8
NSA_HEADS = 8
NSA_KV_GROUPS = 2
NSA_HPG = NSA_HEADS // NSA_KV_GROUPS
CMP_LEN = 32
CMP_STRIDE = 16
SLC_BLOCK = 64
SLC_TOPN = 16
WINDOW = 512
REL_BUCKETS = 32
REL_MAX_DIST = 128
N_GROUPS = 8
EXPERTS_PER_GROUP = 8
N_EXPERTS = N_GROUPS * EXPERTS_PER_GROUP
TOPK_IN_GROUP = 2
RMS_EPS = 1e-6
FORCED_SCORE = 1e4

LANE = 128
SUBLANE = 8
VMEM_LIMIT_BYTES = 56 * 1024 * 1024

NEG = -1e30
QT = 128
MOE_TILE = 256

HS_QA, HS_KA, HS_VA, HS_QB = 0, 8, 16, 24
HS_KCP, HS_VCP, HS_KSL, HS_VSL, HS_KSW, HS_VSW = 32, 34, 36, 38, 40, 42
N_HEAD_SLOTS = 44

_F32 = jnp.float32
_BF16 = jnp.bfloat16


def _cparams(sem):
    return pltpu.CompilerParams(dimension_semantics=sem, vmem_limit_bytes=VMEM_LIMIT_BYTES)


def _dot(a, b):
    return jnp.dot(a, b, preferred_element_type=_F32)


def _dot_nt(a, b):
    return lax.dot_general(a, b, (((1,), (1,)), ((), ())), preferred_element_type=_F32)


def _split_hi_lo(x):
    hi = x.astype(_BF16)
    lo = (x - hi.astype(_F32)).astype(_BF16)
    return hi, lo


def _adaln_kernel(c_ref, w_ref, b_ref, o_ref):
    c = c_ref[...]
    ca = c * jax.nn.sigmoid(c)
    o_ref[...] = jnp.dot(ca, w_ref[...], preferred_element_type=_F32,
                         precision=lax.Precision.HIGHEST) + b_ref[...]


def _adaln(c, w, b):
    bsz, d = c.shape
    n = w.shape[1]
    tn = 1024
    cp = jnp.pad(c, ((0, SUBLANE - bsz), (0, 0)))
    out = pl.pallas_call(
        _adaln_kernel,
        grid=(n // tn,),
        in_specs=[pl.BlockSpec((SUBLANE, d), lambda j: (0, 0)),
                  pl.BlockSpec((d, tn), lambda j: (0, j)),
                  pl.BlockSpec((1, tn), lambda j: (0, j))],
        out_specs=pl.BlockSpec((SUBLANE, tn), lambda j: (0, j)),
        out_shape=jax.ShapeDtypeStruct((SUBLANE, n), _F32),
        compiler_params=_cparams(("arbitrary",)),
        name="adaln",
    )(cp, w, b.reshape(1, n))
    return out[:bsz]


def _nmm_kernel(x_ref, g_ref, sh_ref, sc_ref, w_ref, cs_ref, o_ref, xn_ref, *, heads_out, act):
    @pl.when(pl.program_id(2) == 0)
    def _():
        x = x_ref[0]
        ms = jnp.mean(x * x, axis=-1, keepdims=True)
        y = x * lax.rsqrt(ms + RMS_EPS) * g_ref[...]
        y = y * (1.0 + sc_ref[0]) + sh_ref[0]
        xn_ref[...] = y.astype(_BF16)

    acc = _dot(xn_ref[...], w_ref[...]) * cs_ref[...]
    if act == "sigmoid":
        acc = jax.nn.sigmoid(acc)
    if heads_out:
        for h in range(acc.shape[1] // HEAD_DIM):
            o_ref[0, h] = acc[:, h * HEAD_DIM:(h + 1) * HEAD_DIM].astype(o_ref.dtype)
    else:
        o_ref[0] = acc.astype(o_ref.dtype)


def _norm_mod_matmul(x, g, sh, sc, w, colscale, *, heads_out, act, out_dtype, tm, tn):
    bsz, s, d = x.shape
    n = w.shape[1]
    grid = (bsz, s // tm, n // tn)
    if heads_out:
        out_shape = jax.ShapeDtypeStruct((bsz, n // HEAD_DIM, s, HEAD_DIM), out_dtype)
        out_spec = pl.BlockSpec((1, tn // HEAD_DIM, tm, HEAD_DIM), lambda b, i, j: (b, j, i, 0))
    else:
        out_shape = jax.ShapeDtypeStruct((bsz, s, n), out_dtype)
        out_spec = pl.BlockSpec((1, tm, tn), lambda b, i, j: (b, i, j))
    return pl.pallas_call(
        functools.partial(_nmm_kernel, heads_out=heads_out, act=act),
        grid=grid,
        in_specs=[pl.BlockSpec((1, tm, d), lambda b, i, j: (b, i, 0)),
                  pl.BlockSpec((1, d), lambda b, i, j: (0, 0)),
                  pl.BlockSpec((1, 1, d), lambda b, i, j: (b, 0, 0)),
                  pl.BlockSpec((1, 1, d), lambda b, i, j: (b, 0, 0)),
                  pl.BlockSpec((d, tn), lambda b, i, j: (0, j)),
                  pl.BlockSpec((1, tn), lambda b, i, j: (0, j))],
        out_specs=out_spec,
        out_shape=out_shape,
        scratch_shapes=[pltpu.VMEM((tm, d), _BF16)],
        compiler_params=_cparams(("parallel", "parallel", "arbitrary")),
        name="norm_mod_matmul",
    )(x, g.reshape(1, d), sh.reshape(bsz, 1, d), sc.reshape(bsz, 1, d), w, colscale.reshape(1, n))


def _cmp_kernel(kv_ref, w1c_ref, pe_ref, w1_ref, w2_ref, o_ref):
    kv = kv_ref[0, 0]
    nc = kv.shape[0]
    hdim = w2_ref.shape[0]
    h = _dot(kv, w1c_ref[...])
    c = jnp.dot(pe_ref[...], w1_ref[...], preferred_element_type=_F32,
                precision=lax.Precision.HIGHEST)
    u = h[:, :hdim]
    v = pltpu.roll(h[:, hdim:], nc - 1, 0)
    hid = jax.nn.gelu(u + v + c[0:1])
    out = _dot(hid.astype(_BF16), w2_ref[...])
    row = lax.broadcasted_iota(jnp.int32, out.shape, 0)
    o_ref[0, 0] = jnp.where(row < nc - 1, out, 0.0)


def _compress(kv_chunks, pe, w1, w2):
    bsz, g, nc, ck = kv_chunks.shape
    hdim = w1.shape[1]
    dh = w2.shape[1]
    w1c = jnp.concatenate([w1[:ck], w1[ck:]], axis=1).astype(_BF16)
    pe_p = jnp.pad(pe.reshape(1, -1), ((0, SUBLANE - 1), (0, 0)))
    return pl.pallas_call(
        _cmp_kernel,
        grid=(bsz, g),
        in_specs=[pl.BlockSpec((1, 1, nc, ck), lambda b, gi: (b, gi, 0, 0)),
                  pl.BlockSpec((ck, 2 * hdim), lambda b, gi: (0, 0)),
                  pl.BlockSpec((SUBLANE, 2 * ck), lambda b, gi: (0, 0)),
                  pl.BlockSpec((2 * ck, hdim), lambda b, gi: (0, 0)),
                  pl.BlockSpec((hdim, dh), lambda b, gi: (0, 0))],
        out_specs=pl.BlockSpec((1, 1, nc, dh), lambda b, gi: (b, gi, 0, 0)),
        out_shape=jax.ShapeDtypeStruct((bsz, g, nc, dh), _F32),
        compiler_params=_cparams(("parallel", "parallel")),
        name="compress_blocks",
    )(kv_chunks, w1c, pe_p, w1, w2.astype(_BF16))


SB_TQ = 512
SB_TK = 512
SB_CH = 256
SB_HB = 4
LOG2E = math.log2(math.e)


def _neg_abs(x):
    bits = lax.bitcast_convert_type(x, jnp.uint32) | jnp.uint32(0x80000000)
    return lax.bitcast_convert_type(bits, _F32)


def _sb_kernel(q_ref, k_ref, v_ref, tri_ref, o_ref):
    i = pl.program_id(2)
    tq = q_ref.shape[2]
    tri = tri_ref[...]
    n_sub = SB_TK // SB_CH
    row = lax.broadcasted_iota(jnp.int32, (tq, SB_CH), 0)
    col = lax.broadcasted_iota(jnp.int32, (tq, SB_CH), 1)
    qs = [q_ref[0, h] for h in range(SB_HB)]

    def tile(j, carries, diagonal):
        units = [(c, h) for c in reversed(range(n_sub)) for h in range(SB_HB)]
        r_run = [cr[0] for cr in carries]
        acc = [cr[1] for cr in carries]
        st = [dict() for _ in units]

        def s_qk(u):
            c, h = units[u]
            start = pl.multiple_of(j * SB_TK + c * SB_CH, SB_CH)
            st[u]["start"] = start
            st[u]["z"] = _dot_nt(qs[h], k_ref[0, h, pl.ds(start, SB_CH), :])

        def s_softplus(u):
            c, h = units[u]
            z = st[u]["z"]
            sp = jnp.maximum(z, 0.0) + jnp.log2(1.0 + jnp.exp2(_neg_abs(z)))
            if diagonal:
                st[u]["causal"] = (col + c * SB_CH) < row
                sp = jnp.where(st[u]["causal"], sp, 0.0)
            st[u]["hilo"] = _split_hi_lo(sp)

        def s_suffix(u):
            hi, lo = st[u]["hilo"]
            st[u]["w"] = _dot(hi, tri) + _dot(lo, tri)

        def s_weights(u):
            c, h = units[u]
            w = st[u]["w"]
            a = jnp.exp2(st[u]["z"] + w + r_run[h])
            if diagonal:
                a = jnp.where(st[u]["causal"], a, 0.0)
            st[u]["a"] = a.astype(_BF16)
            r_run[h] = r_run[h] + w[:, 0:1]

        def s_av(u):
            c, h = units[u]
            acc[h] = acc[h] + _dot(st[u]["a"], v_ref[0, h, pl.ds(st[u]["start"], SB_CH), :])

        stages = [s_qk, s_softplus, s_suffix, s_weights, s_av]
        for step in range(len(units) + len(stages) - 1):
            for k, fn in enumerate(stages):
                if 0 <= step - k < len(units):
                    fn(step - k)
        return tuple(zip(r_run, acc))

    init = tuple((jnp.zeros((tq, 1), _F32), jnp.zeros((tq, HEAD_DIM), _F32)) for _ in range(SB_HB))
    carries = tile(i, init, True)
    carries = lax.fori_loop(0, i, lambda it, cr: tile(i - 1 - it, cr, False), carries)
    for h in range(SB_HB):
        o_ref[0, :, h * HEAD_DIM:(h + 1) * HEAD_DIM] = carries[h][1].astype(o_ref.dtype)


def _stick_breaking(heads, bsz, s):
    tq, hb = SB_TQ, SB_HB
    assert s % SB_TK == 0 and SB_TQ == SB_TK and SB_HEADS % hb == 0
    assert HS_QA % hb == 0 and HS_KA % hb == 0 and HS_VA % hb == 0
    jj = np.arange(SB_CH)
    tri = -(jj[:, None] >= jj[None, :]).astype(np.float32)
    return pl.pallas_call(
        _sb_kernel,
        grid=(bsz, SB_HEADS // hb, s // tq),
        in_specs=[pl.BlockSpec((1, hb, tq, HEAD_DIM), lambda b, h, i: (b, HS_QA // hb + h, i, 0)),
                  pl.BlockSpec((1, hb, s, HEAD_DIM), lambda b, h, i: (b, HS_KA // hb + h, 0, 0)),
                  pl.BlockSpec((1, hb, s, HEAD_DIM), lambda b, h, i: (b, HS_VA // hb + h, 0, 0)),
                  pl.BlockSpec((SB_CH, SB_CH), lambda b, h, i: (0, 0))],
        out_specs=pl.BlockSpec((1, tq, hb * HEAD_DIM), lambda b, h, i: (b, i, h)),
        out_shape=jax.ShapeDtypeStruct((bsz, s, SB_HEADS * HEAD_DIM), _BF16),
        compiler_params=_cparams(("parallel", "parallel", "arbitrary")),
        name="stick_breaking_attention",
    )(heads, heads, heads, jnp.asarray(tri, _BF16))


NSA_FAR_TK = 512
NSA_FAR_GROUP = 2
NEAR_CMP = 16
CMP_PAD = 8
MASK_BIG = 2.0 ** 30


def _nsa_kernel(q_ref, kc_ref, vc_ref, ksl_ref, vsl_ref, ksw_ref, vsw_ref, gate_ref,
                bt_ref, bc_ref, ov_ref, o_ref, *, n_sel):
    i = pl.program_id(2)
    hpg = q_ref.shape[1]
    rows = hpg * QT
    s_len = ksw_ref.shape[2]
    q3 = q_ref[0]
    q = q3.reshape(rows, HEAD_DIM)

    def scores(qq, k):
        return _dot_nt(qq, k).reshape(hpg, QT, k.shape[0])

    def pv(p, v):
        return _dot(p.reshape(rows, p.shape[-1]).astype(_BF16), v)

    def lane_max(parts):
        return jnp.max(functools.reduce(jnp.maximum, parts), axis=-1, keepdims=True)

    def lane_sum(parts):
        return jnp.sum(functools.reduce(lambda a, b: a + b, parts), axis=-1, keepdims=True)

    ncp = kc_ref.shape[2] - NEAR_CMP
    kc_far = kc_ref[0, 0, pl.ds(0, ncp), :].astype(_BF16)
    vc_far = vc_ref[0, 0, pl.ds(0, ncp), :].astype(_BF16)
    near0 = pl.multiple_of(i * (QT // CMP_STRIDE), SUBLANE)
    kc_near = kc_ref[0, 0, pl.ds(near0, NEAR_CMP), :].astype(_BF16)
    vc_near = vc_ref[0, 0, pl.ds(near0, NEAR_CMP), :].astype(_BF16)
    p_io = lax.broadcasted_iota(jnp.int32, (QT, ncp), 1)
    far_mask = ((p_io >= CMP_PAD) & (p_io < near0))[None]
    u_io = lax.broadcasted_iota(jnp.int32, (QT, NEAR_CMP), 1)
    ru_io = lax.broadcasted_iota(jnp.int32, (QT, NEAR_CMP), 0)
    d_near = ru_io - CMP_STRIDE * u_io + (CMP_STRIDE * CMP_PAD - CMP_LEN + 1)
    near_mask = ((d_near >= 0) & (near0 + u_io >= CMP_PAD))[None]
    s_far = jnp.where(far_mask, scores(q, kc_far), NEG)
    s_near = jnp.where(near_mask, scores(q, kc_near) + bc_ref[...], NEG)
    m_c = jnp.maximum(jnp.max(s_far, axis=-1, keepdims=True), jnp.max(s_near, axis=-1, keepdims=True))
    e_far = jnp.where(far_mask, jnp.exp2(s_far - m_c), 0.0)
    e_near = jnp.where(near_mask, jnp.exp2(s_near - m_c), 0.0)
    inv_c = 1.0 / jnp.maximum(jnp.sum(e_far, axis=-1, keepdims=True) + jnp.sum(e_near, axis=-1, keepdims=True), 1e-30)
    p_far = e_far * inv_c
    p_near = e_near * inv_c
    o_c = pv(p_far, vc_far) + pv(p_near, vc_near)

    ps_far = jnp.sum(p_far, axis=0)
    ps_near = jnp.sum(p_near, axis=0)
    ov_far = ov_ref[pl.ds(0, ncp), :].astype(_BF16)
    ov_near = ov_ref[pl.ds(near0, NEAR_CMP), :].astype(_BF16)
    fh, fl = _split_hi_lo(ps_far)
    nh, nl = _split_hi_lo(ps_near)
    imp = _dot(fh, ov_far) + _dot(fl, ov_far) + _dot(nh, ov_near) + _dot(nl, ov_near)

    imp_t = imp.T
    j_io = lax.broadcasted_iota(jnp.int32, (LANE, QT), 0)
    t_abs = i * QT + lax.broadcasted_iota(jnp.int32, (LANE, QT), 1)
    cur = t_abs >> int(math.log2(SLC_BLOCK))
    forced = (j_io == 0) | (j_io == cur) | (j_io == cur - 1)
    valid_b = (j_io * SLC_BLOCK) <= t_abs
    work = jnp.where(forced, FORCED_SCORE, jnp.where(valid_b, imp_t, -1.0))
    j_f = j_io.astype(_F32)
    sel_t = jnp.zeros((LANE, QT), _F32)
    for _ in range(n_sel):
        mx = jnp.max(work, axis=0, keepdims=True)
        first = jnp.min(jnp.where(work == mx, j_f, 1e9), axis=0, keepdims=True)
        pick = j_f == first
        sel_t = jnp.where(pick, 1.0, sel_t)
        work = jnp.where(pick, -2.0, work)
    blk_per_q = QT // SLC_BLOCK
    notsel = (1.0 - sel_t).T
    c_blk = lax.broadcasted_iota(jnp.int32, (QT, LANE), 1)
    notsel_far = jnp.where(c_blk < (i - 1) * blk_per_q, notsel, 1.0)

    def extend(ns):
        ns3 = jnp.broadcast_to(ns.astype(_BF16)[None], (hpg, QT, LANE))
        return jnp.concatenate([q3, ns3], axis=2).reshape(rows, HEAD_DIM + LANE)

    q_near = extend(notsel)
    q_far = extend(notsel_far)

    n_w = WINDOW // QT + 1
    sw_parts, sw_v = [], []
    for m in range(n_w):
        start = pl.multiple_of(jnp.maximum(i - m, 0) * QT, QT)
        sw_v.append(vsw_ref[0, 0, pl.ds(start, QT), :])
        sc = scores(q, ksw_ref[0, 0, pl.ds(start, QT), :])
        if m < 2:
            sc = sc + bt_ref[m]
        elif m == n_w - 1:
            sc = sc + bt_ref[2]
        if m > 0:
            sc = sc + jnp.where(i >= m, 0.0, -MASK_BIG)
        sw_parts.append(sc)
    m_w = lane_max(sw_parts)
    e_w = [jnp.exp2(sc - m_w) for sc in sw_parts]
    l_w = lane_sum(e_w)
    o_w = functools.reduce(lambda a, b: a + b, [pv(e, v) for e, v in zip(e_w, sw_v)])
    o_w = o_w * (1.0 / jnp.maximum(l_w, 1e-30)).reshape(rows, 1)

    st_prev = pl.multiple_of(jnp.maximum(i - 1, 0) * QT, QT)
    st_cur = pl.multiple_of(i * QT, QT)
    s_prev = scores(q_near, ksl_ref[0, 0, pl.ds(st_prev, QT), :]) + bt_ref[1] + jnp.where(i >= 1, 0.0, -MASK_BIG)
    s_cur = scores(q_near, ksl_ref[0, 0, pl.ds(st_cur, QT), :]) + bt_ref[0]
    m_s = lane_max([s_prev, s_cur])
    e_prev = jnp.exp2(s_prev - m_s)
    e_cur = jnp.exp2(s_cur - m_s)
    l_s = lane_sum([e_prev, e_cur])
    acc_s = (pv(e_prev, vsl_ref[0, 0, pl.ds(st_prev, QT), :])
             + pv(e_cur, vsl_ref[0, 0, pl.ds(st_cur, QT), :]))

    grp_keys = NSA_FAR_GROUP * NSA_FAR_TK
    n_grp = jnp.maximum(((i - 1) * QT + grp_keys - 1) // grp_keys, 0)
    sub = NSA_FAR_TK // LANE

    def far_group(kg, carry):
        m_run, l_run, acc = carry
        starts = [pl.multiple_of((kg * NSA_FAR_GROUP + a) * NSA_FAR_TK, NSA_FAR_TK) for a in range(NSA_FAR_GROUP)]
        scs = [scores(q_far, ksl_ref[0, 0, pl.ds(st, NSA_FAR_TK), :]) for st in starts]
        for st, sc in zip(starts, scs):
            m_new = jnp.maximum(m_run, lane_max([sc[:, :, a * LANE:(a + 1) * LANE] for a in range(sub)]))
            alpha = jnp.exp2(m_run - m_new)
            e = jnp.exp2(sc - m_new)
            l_run = alpha * l_run + lane_sum([e[:, :, a * LANE:(a + 1) * LANE] for a in range(sub)])
            acc = acc * alpha.reshape(rows, 1) + pv(e, vsl_ref[0, 0, pl.ds(st, NSA_FAR_TK), :])
            m_run = m_new
        return m_run, l_run, acc

    if s_len >= grp_keys:
        m_s, l_s, acc_s = lax.fori_loop(0, n_grp, far_group, (m_s, l_s, acc_s))
    o_s = acc_s * (1.0 / jnp.maximum(l_s, 1e-30)).reshape(rows, 1)

    gt = gate_ref[0]
    for h in range(hpg):
        sl = slice(h * QT, (h + 1) * QT)
        out_h = (gt[:, 3 * h:3 * h + 1] * o_c[sl] + gt[:, 3 * h + 1:3 * h + 2] * o_s[sl]
                 + gt[:, 3 * h + 2:3 * h + 3] * o_w[sl])
        o_ref[0, :, h * HEAD_DIM:(h + 1) * HEAD_DIM] = out_h.astype(o_ref.dtype)


def _t5_bucket_np(dist):
    n = np.maximum(dist, 0)
    max_exact = REL_BUCKETS // 2
    nf = np.maximum(n, 1).astype(np.float64)
    large = max_exact + (np.log(nf / max_exact) / math.log(REL_MAX_DIST / max_exact)
                         * (REL_BUCKETS - max_exact)).astype(np.int64)
    large = np.minimum(large, REL_BUCKETS - 1)
    return np.where(n < max_exact, n, large)


def _bias_tables(rel_bias):
    tbl = rel_bias.astype(_F32) * LOG2E
    delta = tbl - tbl[REL_BUCKETS - 1][None, :]
    r = np.arange(QT)[:, None]
    c = np.arange(QT)[None, :]
    assert np.all(_t5_bucket_np(np.arange(REL_MAX_DIST - REL_BUCKETS // 2 + 1, 4 * QT)) == REL_BUCKETS - 1)
    tabs = []
    for m in range(2):
        d = r - c + QT * m
        tab = jnp.where(jnp.asarray(d >= 0)[:, :, None], delta[_t5_bucket_np(d)], -MASK_BIG)
        tabs.append(tab.transpose(2, 0, 1))
    edge = np.where(c > r, 0.0, -MASK_BIG).astype(np.float32)
    tabs.append(jnp.broadcast_to(jnp.asarray(edge)[None], tabs[0].shape))
    bt = jnp.stack(tabs)
    u = np.arange(NEAR_CMP)[None, :]
    dc = r - CMP_STRIDE * u + (CMP_STRIDE * CMP_PAD - CMP_LEN + 1)
    bc = jnp.where(jnp.asarray(dc >= 0)[:, :, None], delta[_t5_bucket_np(dc)], 0.0).transpose(2, 0, 1)
    return bt, bc


def _nsa(heads, k_cmp, v_cmp, gates, rel_bias, bsz, s):
    g, hpg = NSA_KV_GROUPS, NSA_HPG
    nc = s // CMP_STRIDE
    n_slc = s // SLC_BLOCK
    assert n_slc <= LANE and s % (NSA_FAR_GROUP * NSA_FAR_TK) == 0
    n_sel = min(SLC_TOPN, n_slc)
    pad = ((0, 0), (0, 0), (CMP_PAD, NEAR_CMP - CMP_PAD), (0, 0))
    kcp = jnp.pad(k_cmp, pad)
    vcp = jnp.pad(v_cmp, pad)
    ncp_rows = nc + NEAR_CMP
    bt, bc = _bias_tables(rel_bias)
    n_idx = np.arange(nc)[:, None] * CMP_STRIDE
    j_idx = np.arange(LANE)[None, :] * SLC_BLOCK
    ov = np.clip(np.minimum(n_idx + CMP_LEN, j_idx + SLC_BLOCK) - np.maximum(n_idx, j_idx), 0, None) / CMP_LEN
    ov[nc - 1:] = 0.0
    ov[:, n_slc:] = 0.0
    ov_p = np.zeros((ncp_rows, LANE), np.float32)
    ov_p[CMP_PAD:CMP_PAD + nc] = ov
    ind = np.zeros((s, LANE), np.float32)
    ind[np.arange(s), np.arange(s) // SLC_BLOCK] = -MASK_BIG
    ksl_ext = jnp.concatenate(
        [heads[:, HS_KSL:HS_KSL + g],
         jnp.broadcast_to(jnp.asarray(ind, _BF16)[None, None], (bsz, g, s, LANE))], axis=-1)

    kv_spec = lambda slot: pl.BlockSpec((1, 1, s, HEAD_DIM), lambda b, gi, i: (b, slot + gi, 0, 0))
    return pl.pallas_call(
        functools.partial(_nsa_kernel, n_sel=n_sel),
        grid=(bsz, g, s // QT),
        in_specs=[pl.BlockSpec((1, hpg, QT, HEAD_DIM), lambda b, gi, i: (b, HS_QB // hpg + gi, i, 0)),
                  pl.BlockSpec((1, 1, ncp_rows, HEAD_DIM), lambda b, gi, i: (b, gi, 0, 0)),
                  pl.BlockSpec((1, 1, ncp_rows, HEAD_DIM), lambda b, gi, i: (b, gi, 0, 0)),
                  pl.BlockSpec((1, 1, s, HEAD_DIM + LANE), lambda b, gi, i: (b, gi, 0, 0)),
                  kv_spec(HS_VSL), kv_spec(HS_KSW), kv_spec(HS_VSW),
                  pl.BlockSpec((1, QT, LANE), lambda b, gi, i: (b, i, gi)),
                  pl.BlockSpec((3, hpg, QT, QT), lambda b, gi, i: (0, gi, 0, 0)),
                  pl.BlockSpec((hpg, QT, NEAR_CMP), lambda b, gi, i: (gi, 0, 0)),
                  pl.BlockSpec((ncp_rows, LANE), lambda b, gi, i: (0, 0))],
        out_specs=pl.BlockSpec((1, QT, hpg * HEAD_DIM), lambda b, gi, i: (b, i, gi)),
        out_shape=jax.ShapeDtypeStruct((bsz, s, NSA_HEADS * HEAD_DIM), _BF16),
        compiler_params=_cparams(("parallel", "parallel", "arbitrary")),
        name="nsa_attention",
    )(heads, kcp, vcp, ksl_ext, heads, heads, heads, gates, bt, bc, jnp.asarray(ov_p))


def _merge_kernel(oa_ref, ob_ref, wa_ref, wb_ref, ga_ref, gb_ref, o_ref):
    a = _dot(oa_ref[...], wa_ref[...])
    b = _dot(ob_ref[...], wb_ref[...])
    o_ref[...] = (ga_ref[...].astype(_F32) * a + gb_ref[...].astype(_F32) * b).astype(o_ref.dtype)


def _merge(o_a, o_b, w_a, w_b, gates_ab, tm=512, tn=512):
    t, ka = o_a.shape
    kb = o_b.shape[1]
    d = w_a.shape[1]
    nj = d // tn
    return pl.pallas_call(
        _merge_kernel,
        grid=(t // tm, nj),
        in_specs=[pl.BlockSpec((tm, ka), lambda i, j: (i, 0)),
                  pl.BlockSpec((tm, kb), lambda i, j: (i, 0)),
                  pl.BlockSpec((ka, tn), lambda i, j: (0, j)),
                  pl.BlockSpec((kb, tn), lambda i, j: (0, j)),
                  pl.BlockSpec((tm, tn), lambda i, j: (i, j)),
                  pl.BlockSpec((tm, tn), lambda i, j: (i, j + nj))],
        out_specs=pl.BlockSpec((tm, tn), lambda i, j: (i, j)),
        out_shape=jax.ShapeDtypeStruct((t, d), _BF16),
        compiler_params=_cparams(("parallel", "arbitrary")),
        name="branch_merge",
    )(o_a, o_b, w_a, w_b, gates_ab, gates_ab)


def _outproj_kernel(m_ref, w_ref, x_ref, gt_ref, o_ref):
    o_ref[0] = x_ref[0] + gt_ref[0] * _dot(m_ref[0], w_ref[...])


def _out_proj(merged, w, x, gt, tm=512, tn=512):
    bsz, s, d = x.shape
    k = merged.shape[-1]
    return pl.pallas_call(
        _outproj_kernel,
        grid=(bsz, s // tm, d // tn),
        in_specs=[pl.BlockSpec((1, tm, k), lambda b, i, j: (b, i, 0)),
                  pl.BlockSpec((k, tn), lambda b, i, j: (0, j)),
                  pl.BlockSpec((1, tm, tn), lambda b, i, j: (b, i, j)),
                  pl.BlockSpec((1, 1, tn), lambda b, i, j: (b, 0, j))],
        out_specs=pl.BlockSpec((1, tm, tn), lambda b, i, j: (b, i, j)),
        out_shape=jax.ShapeDtypeStruct((bsz, s, d), _F32),
        compiler_params=_cparams(("parallel", "parallel", "arbitrary")),
        name="out_proj_residual",
    )(merged.reshape(bsz, s, k), w, x, gt.reshape(bsz, 1, d))


def _router_kernel(x_ref, g_ref, sh_ref, sc_ref, w_ref, b_ref, hn_ref, e_ref, wt_ref):
    x = x_ref[0]
    ms = jnp.mean(x * x, axis=-1, keepdims=True)
    hn = x * lax.rsqrt(ms + RMS_EPS) * g_ref[...]
    hn = hn * (1.0 + sc_ref[0]) + sh_ref[0]
    hn_ref[0] = hn
    logits = jnp.dot(hn, w_ref[...], preferred_element_type=_F32,
                     precision=lax.Precision.HIGHEST) + b_ref[...]
    lane = lax.broadcasted_iota(jnp.int32, logits.shape, 1)
    lane_f = lane.astype(_F32)

    def first_max(vals):
        mx = jnp.max(vals, axis=-1, keepdims=True)
        idx = jnp.min(jnp.where(vals == mx, lane_f, 1e9), axis=-1, keepdims=True)
        return mx, idx

    is_grp = lane < N_GROUPS
    lg = jnp.where(is_grp, logits, NEG)
    eg = jnp.where(is_grp, jnp.exp(lg - jnp.max(lg, axis=-1, keepdims=True)), 0.0)
    pg = eg / jnp.sum(eg, axis=-1, keepdims=True)
    pg_sel, g_sel = first_max(jnp.where(is_grp, pg, -1.0))
    lo_lane = N_GROUPS + EXPERTS_PER_GROUP * g_sel
    in_grp = (lane_f >= lo_lane) & (lane_f < lo_lane + EXPERTS_PER_GROUP)
    le = jnp.where(in_grp, logits, NEG)
    ee = jnp.where(in_grp, jnp.exp(le - jnp.max(le, axis=-1, keepdims=True)), 0.0)
    pe = ee / jnp.sum(ee, axis=-1, keepdims=True)
    cand = jnp.where(in_grp, pe, -1.0)
    w0, i0 = first_max(cand)
    w1, i1 = first_max(jnp.where(lane_f == i0, -1.0, cand))
    den = w0 + w1
    w0n = w0 / den * pg_sel
    w1n = w1 / den * pg_sel
    e_ref[0] = jnp.where(lane == 0, i0 - N_GROUPS, jnp.where(lane == 1, i1 - N_GROUPS, 0.0)).astype(jnp.int32)
    wt_ref[0] = jnp.where(lane == 0, w0n, jnp.where(lane == 1, w1n, 0.0))


def _router(x1, g, sh, sc, w_grp, b_grp, w_exp, b_exp, tm=256):
    bsz, s, d = x1.shape
    nl = N_GROUPS + N_EXPERTS
    wr = jnp.pad(jnp.concatenate([w_grp, w_exp], axis=1), ((0, 0), (0, LANE - nl)))
    br = jnp.pad(jnp.concatenate([b_grp, b_exp]), (0, LANE - nl)).reshape(1, LANE)
    spec_t = lambda w: pl.BlockSpec((1, tm, w), lambda b, i: (b, i, 0))
    return pl.pallas_call(
        _router_kernel,
        grid=(bsz, s // tm),
        in_specs=[spec_t(d),
                  pl.BlockSpec((1, d), lambda b, i: (0, 0)),
                  pl.BlockSpec((1, 1, d), lambda b, i: (b, 0, 0)),
                  pl.BlockSpec((1, 1, d), lambda b, i: (b, 0, 0)),
                  pl.BlockSpec((d, LANE), lambda b, i: (0, 0)),
                  pl.BlockSpec((1, LANE), lambda b, i: (0, 0))],
        out_specs=[spec_t(d), spec_t(LANE), spec_t(LANE)],
        out_shape=[jax.ShapeDtypeStruct((bsz, s, d), _F32),
                   jax.ShapeDtypeStruct((bsz, s, LANE), jnp.int32),
                   jax.ShapeDtypeStruct((bsz, s, LANE), _F32)],
        compiler_params=_cparams(("parallel", "arbitrary")),
        name="norm_mod_router",
    )(x1, g.reshape(1, d), sh.reshape(bsz, 1, d), sc.reshape(bsz, 1, d), wr, br)


def _row_copy(src, dst, src_row, dst_row, sem):
    return pltpu.make_async_copy(src.at[pl.ds(src_row, 1)], dst.at[pl.ds(dst_row, 1)], sem)


def _expert_kernel(blk_e_ref, nu_ref, tok0_ref, tokn_ref, dstp_ref, hn_hbm, rw_ref, w1_ref, w3_ref, w2_ref,
                   out_hbm, xs_ref, y_ref, gsem, ssem):
    del blk_e_ref
    i = pl.program_id(0)
    n_used = nu_ref[0]
    slot = lax.rem(i, 2)
    nslot = 1 - slot
    tile = xs_ref.shape[1]

    def gather_start(tok_ref, sl):
        for r in range(tile):
            _row_copy(hn_hbm, xs_ref.at[sl], tok_ref[0, 0, r], r, gsem.at[sl]).start()

    def gather_wait(sl):
        for r in range(tile):
            _row_copy(hn_hbm, xs_ref.at[sl], 0, r, gsem.at[sl]).wait()

    def scatter_start(sl):
        for r in range(tile):
            _row_copy(y_ref.at[sl], out_hbm, r, dstp_ref[0, 0, r], ssem.at[sl]).start()

    def scatter_wait(sl):
        for r in range(tile):
            _row_copy(y_ref.at[sl], out_hbm, r, 0, ssem.at[sl]).wait()

    @pl.when(i == 0)
    def _():
        y_ref[1] = jnp.zeros(y_ref.shape[1:], y_ref.dtype)
        gather_start(tok0_ref, 0)

    @pl.when(i <= n_used)
    def _():
        gather_wait(slot)

    @pl.when((i >= 1) & (i - 2 < n_used))
    def _():
        scatter_wait(slot)

    @pl.when(i < n_used)
    def _():
        gather_start(tokn_ref, nslot)
        scatter_start(nslot)
        xs = xs_ref[slot].astype(_BF16)
        h1 = _dot(xs, w1_ref[0])
        h3 = _dot(xs, w3_ref[0])
        hid = (h1 * jax.nn.sigmoid(h1)) * h3
        y_ref[slot] = _dot(hid.astype(_BF16), w2_ref[0]) * rw_ref[...]

    @pl.when(i == n_used)
    def _():
        scatter_start(nslot)


def _experts(hn2d, tables, w1, w3, w2):
    blk_e, n_used, row_tok, row_dst, row_w = tables
    t, d = hn2d.shape
    n_blk = blk_e.shape[0]
    hid = w1.shape[-1]
    smem_blk = lambda imap: pl.BlockSpec((1, 1, MOE_TILE), imap, memory_space=pltpu.SMEM)
    grid_spec = pltpu.PrefetchScalarGridSpec(
        num_scalar_prefetch=2,
        grid=(n_blk,),
        in_specs=[smem_blk(lambda i, be, nu: (0, 0, 0)),
                  smem_blk(lambda i, be, nu: (jnp.minimum(i + 1, n_blk - 1), 0, 0)),
                  smem_blk(lambda i, be, nu: (i, 0, 0)),
                  pl.BlockSpec(memory_space=pl.ANY),
                  pl.BlockSpec((MOE_TILE, 1), lambda i, be, nu: (i, 0)),
                  pl.BlockSpec((1, d, hid), lambda i, be, nu: (be[i], 0, 0)),
                  pl.BlockSpec((1, d, hid), lambda i, be, nu: (be[i], 0, 0)),
                  pl.BlockSpec((1, hid, d), lambda i, be, nu: (be[i], 0, 0))],
        out_specs=pl.BlockSpec(memory_space=pl.ANY),
        scratch_shapes=[pltpu.VMEM((2, MOE_TILE, d), _F32), pltpu.VMEM((2, MOE_TILE, d), _F32),
                        pltpu.SemaphoreType.DMA((2,)), pltpu.SemaphoreType.DMA((2,))],
    )
    return pl.pallas_call(
        _expert_kernel,
        grid_spec=grid_spec,
        out_shape=jax.ShapeDtypeStruct((TOPK_IN_GROUP * t + MOE_TILE, d), _F32),
        compiler_params=_cparams(("arbitrary",)),
        name="expert_mlp",
    )(blk_e, n_used, row_tok, row_tok, row_dst, hn2d, row_w, w1, w3, w2)


CMB_TM = 512


def _combine_kernel(y0_ref, y1_ref, x_ref, gt_ref, g_ref, o_ref):
    x2 = x_ref[0] + gt_ref[0] * (y0_ref[...] + y1_ref[...])
    ms = jnp.mean(x2 * x2, axis=-1, keepdims=True)
    o_ref[0] = x2 * lax.rsqrt(ms + RMS_EPS) * g_ref[...]


def _combine(y, x1, gt, g):
    bsz, s, d = x1.shape
    tm = CMB_TM
    n_t = (bsz * s) // tm
    per_b = s // tm
    return pl.pallas_call(
        _combine_kernel,
        grid=(n_t,),
        in_specs=[pl.BlockSpec((tm, d), lambda i: (i, 0)),
                  pl.BlockSpec((tm, d), lambda i: (i + n_t, 0)),
                  pl.BlockSpec((1, tm, d), lambda i: (i // per_b, i % per_b, 0)),
                  pl.BlockSpec((1, 1, d), lambda i: (i // per_b, 0, 0)),
                  pl.BlockSpec((1, d), lambda i: (0, 0))],
        out_specs=pl.BlockSpec((1, tm, d), lambda i: (i // per_b, i % per_b, 0)),
        out_shape=jax.ShapeDtypeStruct((bsz, s, d), _F32),
        compiler_params=_cparams(("arbitrary",)),
        name="moe_combine_final_norm",
    )(y, y, x1, gt.reshape(bsz, 1, d), g.reshape(1, d))


def _dispatch_tables(e_id, w_top, t):
    n = t * TOPK_IN_GROUP
    e_flat = e_id.reshape(n)
    w_flat = w_top.reshape(n)
    order = jnp.argsort(e_flat).astype(jnp.int32)
    counts = jnp.sum((e_flat[:, None] == jnp.arange(N_EXPERTS, dtype=e_flat.dtype)[None, :]).astype(jnp.int32), axis=0)
    start = jnp.cumsum(counts) - counts
    padded = (counts + MOE_TILE - 1) // MOE_TILE * MOE_TILE
    pend = jnp.cumsum(padded)
    pstart = pend - padded
    n_blk = (n + N_EXPERTS * (MOE_TILE - 1) + MOE_TILE - 1) // MOE_TILE + 2
    blk_first = jnp.arange(n_blk, dtype=jnp.int32) * MOE_TILE
    blk_e = jnp.minimum(jnp.sum((pend[None, :] <= blk_first[:, None]).astype(jnp.int32), axis=1), N_EXPERTS - 1)
    n_used = (pend[-1] // MOE_TILE).astype(jnp.int32).reshape(1)
    off = (blk_first - pstart[blk_e])[:, None] + jnp.arange(MOE_TILE, dtype=jnp.int32)[None, :]
    valid = off < counts[blk_e][:, None]
    a_row = order[jnp.clip(start[blk_e][:, None] + off, 0, n - 1)]
    row_tok = jnp.where(valid, a_row // TOPK_IN_GROUP, 0).astype(jnp.int32)
    spare = (n + jnp.arange(MOE_TILE, dtype=jnp.int32))[None, :]
    row_dst = jnp.where(valid, (a_row % TOPK_IN_GROUP) * t + a_row // TOPK_IN_GROUP, spare).astype(jnp.int32)
    row_w = jnp.where(valid, w_flat[a_row], 0.0)
    dst_prev = jnp.concatenate([spare, row_dst[:-1]], axis=0)
    return (blk_e.astype(jnp.int32), n_used, row_tok.reshape(n_blk, 1, MOE_TILE),
            dst_prev.reshape(n_blk, 1, MOE_TILE), row_w.reshape(n_blk * MOE_TILE, 1))


def kernel(x, c, ada_w, ada_b, norm1_g, norm2_g, normf_g, w_in, rel_bias, cmp_pe_k, cmp_w1_k, cmp_w2_k,
           cmp_pe_v, cmp_w1_v, cmp_w2_v, w_branch_a, w_branch_b, w_out, router_w_grp, router_b_grp,
           router_w_exp, router_b_exp, expert_w1, expert_w3, expert_w2):
    bsz, s, d = x.shape
    depth = ada_w.shape[0]
    assert depth == 1, "the final norm is fused into the MoE combine of a single layer"
    dh = HEAD_DIM
    n_hcols = N_HEAD_SLOTS * dh
    n_gate3 = 3 * NSA_HEADS
    per_g = n_gate3 // NSA_KV_GROUPS
    scale = dh ** -0.5
    tm = min(512, s)
    for layer in range(depth):
        mod = _adaln(c, ada_w[layer], ada_b[layer])
        sh1, sc1, gt1, sh2, sc2, gt2 = jnp.split(mod, 6, axis=-1)
        wl = w_in[layer]
        cs = np.ones((n_hcols,), np.float32)
        cs[HS_QA * dh:(HS_QA + SB_HEADS) * dh] = scale * LOG2E
        cs[HS_QB * dh:(HS_QB + NSA_HEADS) * dh] = scale * LOG2E
        heads = _norm_mod_matmul(x, norm1_g[layer], sh1, sc1, wl[:, :n_hcols].astype(_BF16), jnp.asarray(cs),
                                 heads_out=True, act=None, out_dtype=_BF16, tm=tm, tn=512)
        wg = jnp.zeros((d, NSA_KV_GROUPS * LANE), w_in.dtype)
        for gi in range(NSA_KV_GROUPS):
            wg = wg.at[:, gi * LANE:gi * LANE + per_g].set(
                wl[:, n_hcols + gi * per_g:n_hcols + (gi + 1) * per_g])
        gates3 = _norm_mod_matmul(x, norm1_g[layer], sh1, sc1, wg.astype(_BF16),
                                  jnp.ones((NSA_KV_GROUPS * LANE,), _F32),
                                  heads_out=False, act="sigmoid", out_dtype=_F32, tm=tm, tn=NSA_KV_GROUPS * LANE)
        gates_ab = _norm_mod_matmul(x, norm1_g[layer], sh1, sc1, wl[:, n_hcols + n_gate3:].astype(_BF16),
                                    jnp.ones((2 * d,), _F32),
                                    heads_out=False, act="sigmoid", out_dtype=_BF16, tm=tm, tn=512)

        def chunks(slot):
            return heads[:, slot:slot + NSA_KV_GROUPS].reshape(bsz, NSA_KV_GROUPS, s // CMP_STRIDE, CMP_STRIDE * dh)

        k_cmp = _compress(chunks(HS_KCP), cmp_pe_k[layer], cmp_w1_k[layer], cmp_w2_k[layer])
        v_cmp = _compress(chunks(HS_VCP), cmp_pe_v[layer], cmp_w1_v[layer], cmp_w2_v[layer])
        o_a = _stick_breaking(heads, bsz, s)
        o_b = _nsa(heads, k_cmp, v_cmp, gates3, rel_bias, bsz, s)
        merged = _merge(o_a.reshape(bsz * s, -1), o_b.reshape(bsz * s, -1),
                        w_branch_a[layer].astype(_BF16), w_branch_b[layer].astype(_BF16),
                        gates_ab.reshape(bsz * s, 2 * d), tm=tm)
        x = _out_proj(merged, w_out[layer].astype(_BF16), x, gt1, tm=tm)
        hn, e_l, w_l = _router(x, norm2_g[layer], sh2, sc2, router_w_grp[layer], router_b_grp[layer],
                               router_w_exp[layer], router_b_exp[layer], tm=min(256, s))
        t = bsz * s
        tables = _dispatch_tables(e_l.reshape(t, LANE)[:, :TOPK_IN_GROUP],
                                  w_l.reshape(t, LANE)[:, :TOPK_IN_GROUP], t)
        y = _experts(hn.reshape(t, d), tables, expert_w1[layer].astype(_BF16),
                     expert_w3[layer].astype(_BF16), expert_w2[layer].astype(_BF16))
        x = _combine(y, x, gt2, normf_g)
    return x
```
